```python
import math
import jax, jax.numpy as jnp
from jax import lax
import numpy as np

D_MODEL = 2048
BATCH = 16
SEQ = 2048
DEPTH = 2

MEM_LEN = 256
N_EVEN = (DEPTH + 1) // 2
N_ODD = DEPTH // 2
D_CONV = D_MODEL // 2
CONV_WIDTH = 31
D_DIFF = D_MODEL // 2
DIFF_DH = 64
DIFF_DV = 2 * DIFF_DH
DIFF_HEADS = D_DIFF // DIFF_DV
Q_BLOCK = 128
S5_GROUP_CH = 16
S5_GROUPS = D_MODEL // S5_GROUP_CH
S5_STATE = 64
S5_DT_MIN = 1e-3
S5_DT_MAX = 1e-1
XA_HEADS = 4
XA_DH = D_MODEL // XA_HEADS
D_FF = -(-8 * D_MODEL // (3 * 256)) * 256
FFN_CONV_WIDTH = 3
RMS_EPS = 1e-6
LN_EPS = 1e-5

kernel_name = "hybrid_conv_diffattn_s5_encoder"


def rms_norm(x, g, eps=RMS_EPS):
    xf = x.astype(jnp.float32)
    y = xf * lax.rsqrt(jnp.mean(xf * xf, axis=-1, keepdims=True) + eps)
    return (y * g.astype(jnp.float32)).astype(x.dtype)


def layer_norm(x, g, b, eps=LN_EPS):
    xf = x.astype(jnp.float32)
    mu = jnp.mean(xf, axis=-1, keepdims=True)
    var = jnp.mean(jnp.square(xf - mu), axis=-1, keepdims=True)
    y = (xf - mu) * lax.rsqrt(var + eps) * g.astype(jnp.float32) + b.astype(jnp.float32)
    return y.astype(x.dtype)


def depthwise_conv(x, w, b):
    y = lax.conv_general_dilated(
        x, w[:, None, :].astype(x.dtype), window_strides=(1,), padding="SAME",
        dimension_numbers=("NWC", "WIO", "NWC"), feature_group_count=x.shape[-1])
    return y + b.astype(x.dtype)


def alibi_slopes(n_heads):
    return jnp.exp2(-8.0 * jnp.arange(1, n_heads + 1, dtype=jnp.float32) / n_heads)


def conformer_conv(val, gate, dw, db, ln_g, ln_b):
    h = val * jax.nn.sigmoid(gate)
    h = depthwise_conv(h, dw, db)
    h = layer_norm(h, ln_g, ln_b)
    return jax.nn.silu(h)


def diff_attention(q, k, v, lq1, lk1, lq2, lk2, subln_g, lambda_init):
    bsz, s_len = q.shape[0], q.shape[1]
    lam = (jnp.exp(jnp.sum(lq1.astype(jnp.float32) * lk1.astype(jnp.float32)))
           - jnp.exp(jnp.sum(lq2.astype(jnp.float32) * lk2.astype(jnp.float32))) + lambda_init)
    slopes = alibi_slopes(DIFF_HEADS)
    pos = jnp.arange(s_len, dtype=jnp.float32)
    n_blk = s_len // Q_BLOCK
    qb = q.reshape(bsz, n_blk, Q_BLOCK, DIFF_HEADS, 2, DIFF_DH).transpose(1, 0, 2, 3, 4, 5)
    pb = pos.reshape(n_blk, Q_BLOCK)
    scale = DIFF_DH ** -0.5

    def one_block(args):
        qi, pi = args
        s = jnp.einsum("bqhcd,bkhcd->bchqk", qi, k).astype(jnp.float32) * scale
        bias = -slopes[:, None, None] * jnp.abs(pi[:, None] - pos[None, :])
        p = jax.nn.softmax(s + bias[None, None], axis=-1)
        w = (p[:, 0] - lam * p[:, 1]).astype(v.dtype)
        return jnp.einsum("bhqk,bkhe->bqhe", w, v)

    o = lax.map(one_block, (qb, pb))
    o = o.transpose(1, 0, 2, 3, 4).reshape(bsz, s_len, DIFF_HEADS, DIFF_DV)
    o = rms_norm(o, subln_g, LN_EPS) * (1.0 - lambda_init)
    return o.reshape(bsz, s_len, DIFF_HEADS * DIFF_DV)


def mixer_conv_diff(xn, w_in, conv_dw, conv_db, conv_ln_g, conv_ln_b,
                    lq1, lk1, lq2, lk2, subln_g, w_out, lambda_init):
    bsz, s_len, _ = xn.shape
    z = xn @ w_in
    a_val, a_gate, q, k, v = jnp.split(
        z, [D_CONV, 2 * D_CONV, 2 * D_CONV + D_DIFF, 2 * D_CONV + 2 * D_DIFF], axis=-1)
    a_out = conformer_conv(a_val, a_gate, conv_dw, conv_db, conv_ln_g, conv_ln_b)
    b_out = diff_attention(
        q.reshape(bsz, s_len, DIFF_HEADS, 2, DIFF_DH),
        k.reshape(bsz, s_len, DIFF_HEADS, 2, DIFF_DH),
        v.reshape(bsz, s_len, DIFF_HEADS, DIFF_DV),
        lq1, lk1, lq2, lk2, subln_g, lambda_init)
    return jnp.concatenate([a_out, b_out], axis=-1) @ w_out


def s5_discretize(lam_re, lam_im, log_dt, b_re, b_im):
    dt = jnp.exp(log_dt.astype(jnp.float32))[:, None]
    lr = lam_re.astype(jnp.float32)
    li = lam_im.astype(jnp.float32)
    mag = jnp.exp(lr * dt)
    lb_re = mag * jnp.cos(li * dt)
    lb_im = mag * jnp.sin(li * dt)
    den = lr * lr + li * li
    f_re = ((lb_re - 1.0) * lr + lb_im * li) / den
    f_im = (lb_im * lr - (lb_re - 1.0) * li) / den
    br = b_re.astype(jnp.float32)
    bi = b_im.astype(jnp.float32)
    bb_re = f_re[..., None] * br - f_im[..., None] * bi
    bb_im = f_re[..., None] * bi + f_im[..., None] * br
    return lb_re, lb_im, bb_re, bb_im


def _linear_recurrence_combine(e1, e2):
    a1r, a1i, b1r, b1i = e1
    a2r, a2i, b2r, b2i = e2
    return (a2r * a1r - a2i * a1i,
            a2r * a1i + a2i * a1r,
            a2r * b1r - a2i * b1i + b2r,
            a2r * b1i + a2i * b1r + b2i)


def mixer_s5(xn, lam_re, lam_im, log_dt, b_re, b_im, c_re, c_im, d_skip, w_val, w_gate):
    bsz, s_len, _ = xn.shape
    u = xn.reshape(bsz, s_len, S5_GROUPS, S5_GROUP_CH).astype(jnp.float32)
    disc = [s5_discretize(lam_re[dr], lam_im[dr], log_dt[dr], b_re[dr], b_im[dr]) for dr in (0, 1)]
    cr = c_re.astype(jnp.float32)
    ci = c_im.astype(jnp.float32)

    def one_sequence(us):
        outs = []
        for dr, rev in ((0, False), (1, True)):
            lb_re, lb_im, bb_re, bb_im = disc[dr]
            bu_re = jnp.einsum("sgc,gpc->sgp", us, bb_re)
            bu_im = jnp.einsum("sgc,gpc->sgp", us, bb_im)
            a_re = jnp.broadcast_to(lb_re, bu_re.shape)
            a_im = jnp.broadcast_to(lb_im, bu_re.shape)
            _, _, h_re, h_im = lax.associative_scan(
                _linear_recurrence_combine, (a_re, a_im, bu_re, bu_im), reverse=rev, axis=0)
            outs.append(jnp.einsum("sgp,gcp->sgc", h_re, cr[dr])
                        - jnp.einsum("sgp,gcp->sgc", h_im, ci[dr]))
        return outs[0] + outs[1]

    y = lax.map(one_sequence, u).reshape(bsz, s_len, D_MODEL).astype(xn.dtype)
    y = y + d_skip * xn
    g = jax.nn.gelu(y)
    return (g @ w_val) * jax.nn.sigmoid(g @ w_gate)


def memory_cross_attention(xn, memn, wq, wk, wv, wo):
    bsz, s_len, _ = xn.shape
    m_len = memn.shape[1]
    q = (xn @ wq).reshape(bsz, s_len, XA_HEADS, XA_DH)
    k = (memn @ wk).reshape(bsz, m_len, XA_HEADS, XA_DH)
    v = (memn @ wv).reshape(bsz, m_len, XA_HEADS, XA_DH)
    s = jnp.einsum("bqhd,bkhd->bhqk", q, k).astype(jnp.float32) * (XA_DH ** -0.5)
    p = jax.nn.softmax(s, axis=-1).astype(v.dtype)
    o = jnp.einsum("bhqk,bkhd->bqhd", p, v).reshape(bsz, s_len, D_MODEL)
    return o @ wo


def conv_ffn(xn, w_up, dw, db, w_down):
    h = depthwise_conv(xn @ w_up, dw, db)
    g, v = jnp.split(h, 2, axis=-1)
    return (jax.nn.silu(g) * v) @ w_down


def setup_inputs(seed: int = 0) -> dict:
    key = jax.random.key(seed)
    ks = list(jax.random.split(key, 48))

    def nrm(shape, scale):
        return scale * jax.random.normal(ks.pop(), shape, jnp.float32)

    def gain(shape):
        return 1.0 + nrm(shape, 0.05)

    n_in = 2 * D_CONV + 3 * D_DIFF
    lam_im_base = jnp.pi * jnp.arange(S5_STATE, dtype=jnp.float32)
    log_dt = jax.random.uniform(ks.pop(), (N_ODD, 2, S5_GROUPS), jnp.float32,
                                math.log(S5_DT_MIN), math.log(S5_DT_MAX))
    return {
        "x": nrm((BATCH, SEQ, D_MODEL), 1.0),
        "mem": nrm((BATCH, MEM_LEN, D_MODEL), 1.0),
        "norm_mix_g": gain((DEPTH, D_MODEL)),
        "ab_w_in": nrm((N_EVEN, D_MODEL, n_in), D_MODEL ** -0.5),
        "conv_dw": nrm((N_EVEN, CONV_WIDTH, D_CONV), CONV_WIDTH ** -0.5),
        "conv_db": nrm((N_EVEN, D_CONV), 0.02),
        "conv_ln_g": gain((N_EVEN, D_CONV)),
        "conv_ln_b": nrm((N_EVEN, D_CONV), 0.02),
        "diff_lq1": nrm((N_EVEN, DIFF_DH), 0.1),
        "diff_lk1": nrm((N_EVEN, DIFF_DH), 0.1),
        "diff_lq2": nrm((N_EVEN, DIFF_DH), 0.1),
        "diff_lk2": nrm((N_EVEN, DIFF_DH), 0.1),
        "diff_subln_g": gain((N_EVEN, DIFF_DV)),
        "ab_w_out": nrm((N_EVEN, D_CONV + D_DIFF, D_MODEL), (D_CONV + D_DIFF) ** -0.5),
        "s5_lam_re": -0.5 + nrm((N_ODD, 2, S5_GROUPS, S5_STATE), 0.01),
        "s5_lam_im": lam_im_base + nrm((N_ODD, 2, S5_GROUPS, S5_STATE), 0.01),
        "s5_log_dt": log_dt,
        "s5_b_re": nrm((N_ODD, 2, S5_GROUPS, S5_STATE, S5_GROUP_CH), (2 * S5_GROUP_CH) ** -0.5),
        "s5_b_im": nrm((N_ODD, 2, S5_GROUPS, S5_STATE, S5_GROUP_CH), (2 * S5_GROUP_CH) ** -0.5),
        "s5_c_re": nrm((N_ODD, 2, S5_GROUPS, S5_GROUP_CH, S5_STATE), (2 * S5_STATE) ** -0.5),
        "s5_c_im": nrm((N_ODD, 2, S5_GROUPS, S5_GROUP_CH, S5_STATE), (2 * S5_STATE) ** -0.5),
        "s5_d": nrm((N_ODD, D_MODEL), 1.0),
        "s5_w_val": nrm((N_ODD, D_MODEL, D_MODEL), D_MODEL ** -0.5),
        "s5_w_gate": nrm((N_ODD, D_MODEL, D_MODEL), D_MODEL ** -0.5),
        "norm_xa_g": gain((DEPTH, D_MODEL)),
        "norm_mem_g": gain((DEPTH, D_MODEL)),
        "xa_wq": nrm((DEPTH, D_MODEL, D_MODEL), D_MODEL ** -0.5),
        "xa_wk": nrm((DEPTH, D_MODEL, D_MODEL), D_MODEL ** -0.5),
        "xa_wv": nrm((DEPTH, D_MODEL, D_MODEL), D_MODEL ** -0.5),
        "xa_wo": nrm((DEPTH, D_MODEL, D_MODEL), D_MODEL ** -0.5),
        "norm_ffn_g": gain((DEPTH, D_MODEL)),
        "ffn_w_up": nrm((DEPTH, D_MODEL, 2 * D_FF), D_MODEL ** -0.5),
        "ffn_dw": nrm((DEPTH, FFN_CONV_WIDTH, 2 * D_FF), FFN_CONV_WIDTH ** -0.5),
        "ffn_db": nrm((DEPTH, 2 * D_FF), 0.02),
        "ffn_w_down": nrm((DEPTH, D_FF, D_MODEL), D_FF ** -0.5),
        "final_g": gain((D_MODEL,)),
    }


def reference(x, mem, norm_mix_g, ab_w_in, conv_dw, conv_db, conv_ln_g, conv_ln_b,
              diff_lq1, diff_lk1, diff_lq2, diff_lk2, diff_subln_g, ab_w_out,
              s5_lam_re, s5_lam_im, s5_log_dt, s5_b_re, s5_b_im, s5_c_re, s5_c_im, s5_d,
              s5_w_val, s5_w_gate, norm_xa_g, norm_mem_g, xa_wq, xa_wk, xa_wv, xa_wo,
              norm_ffn_g, ffn_w_up, ffn_dw, ffn_db, ffn_w_down, final_g):
    for layer in range(DEPTH):
        h = rms_norm(x, norm_mix_g[layer])
        i = layer // 2
        if layer % 2 == 0:
            lambda_init = 0.8 - 0.6 * math.exp(-0.3 * layer)
            x = x + mixer_conv_diff(h, ab_w_in[i], conv_dw[i], conv_db[i], conv_ln_g[i], conv_ln_b[i],
                                    diff_lq1[i], diff_lk1[i], diff_lq2[i], diff_lk2[i],
                                    diff_subln_g[i], ab_w_out[i], lambda_init)
        else:
            x = x + mixer_s5(h, s5_lam_re[i], s5_lam_im[i], s5_log_dt[i], s5_b_re[i], s5_b_im[i],
                             s5_c_re[i], s5_c_im[i], s5_d[i], s5_w_val[i], s5_w_gate[i])
        x = x + memory_cross_attention(rms_norm(x, norm_xa_g[layer]), rms_norm(mem, norm_mem_g[layer]),
                                       xa_wq[layer], xa_wk[layer], xa_wv[layer], xa_wo[layer])
        x = x + conv_ffn(rms_norm(x, norm_ffn_g[layer]), ffn_w_up[layer], ffn_dw[layer],
                         ffn_db[layer], ffn_w_down[layer])
    return rms_norm(x, final_g)
```

```python
import functools
import math

import jax
import jax.numpy as jnp
from jax import lax
from jax.experimental import pallas as pl
from jax.experimental.pallas import tpu as pltpu

RMS_EPS = 1e-6
LN_EPS = 1e-5
XA_HEADS = 4
S5_CHUNK = 16
V7X_VMEM_BUDGET = 56 * 1024 * 1024
BF16 = jnp.bfloat16
F32 = jnp.float32


def _params(semantics, vmem=V7X_VMEM_BUDGET):
    return pltpu.CompilerParams(dimension_semantics=semantics, vmem_limit_bytes=vmem)


def _rms(x, g, eps):
    return x * lax.rsqrt(jnp.mean(x * x, axis=-1, keepdims=True) + eps) * g


def _norm_matmul_kernel(x_ref, g_ref, w_ref, o_ref, xn_ref):
    @pl.when(pl.program_id(1) == 0)
    def _():
        xn_ref[...] = _rms(x_ref[...], g_ref[...], RMS_EPS).astype(xn_ref.dtype)

    o_ref[...] = jnp.dot(xn_ref[...], w_ref[...], preferred_element_type=F32).astype(o_ref.dtype)


def norm_matmul(x, g, w, *, tm, tn):
    m, k = x.shape
    n = w.shape[1]
    return pl.pallas_call(
        _norm_matmul_kernel,
        grid=(m // tm, n // tn),
        in_specs=[pl.BlockSpec((tm, k), lambda i, j: (i, 0)),
                  pl.BlockSpec((1, k), lambda i, j: (0, 0)),
                  pl.BlockSpec((k, tn), lambda i, j: (0, j))],
        out_specs=pl.BlockSpec((tm, tn), lambda i, j: (i, j)),
        out_shape=jax.ShapeDtypeStruct((m, n), BF16),
        scratch_shapes=[pltpu.VMEM((tm, k), BF16)],
        compiler_params=_params(("parallel", "arbitrary")),
        name="norm_matmul",
    )(x, g.reshape(1, k), w)


def _norm_kernel(x_ref, g_ref, o_ref):
    o_ref[...] = _rms(x_ref[...], g_ref[...], RMS_EPS).astype(o_ref.dtype)


def norm_rows(x, g, *, tm, dtype):
    m, k = x.shape
    return pl.pallas_call(
        _norm_kernel,
        grid=(m // tm,),
        in_specs=[pl.BlockSpec((tm, k), lambda i: (i, 0)), pl.BlockSpec((1, k), lambda i: (0, 0))],
        out_specs=pl.BlockSpec((tm, k), lambda i: (i, 0)),
        out_shape=jax.ShapeDtypeStruct((m, k), dtype),
        compiler_params=_params(("parallel",)),
        name="norm_rows",
    )(x, g.reshape(1, k))


def _matmul_residual_kernel(*refs, n_lhs):
    a_refs, w_refs = refs[:n_lhs], refs[n_lhs:2 * n_lhs]
    r_ref, o_ref = refs[2 * n_lhs], refs[2 * n_lhs + 1]
    acc = r_ref[...]
    for a_ref, w_ref in zip(a_refs, w_refs):
        acc = acc + jnp.dot(a_ref[...], w_ref[...], preferred_element_type=F32)
    o_ref[...] = acc


def matmul_residual(lhs, ws, res, *, tm, tn):
    m, n = res.shape
    in_specs = [pl.BlockSpec((tm, a.shape[1]), lambda i, j: (i, 0)) for a in lhs]
    in_specs += [pl.BlockSpec((w.shape[0], tn), lambda i, j: (0, j)) for w in ws]
    in_specs += [pl.BlockSpec((tm, tn), lambda i, j: (i, j))]
    return pl.pallas_call(
        functools.partial(_matmul_residual_kernel, n_lhs=len(lhs)),
        grid=(m // tm, n // tn),
        in_specs=in_specs,
        out_specs=pl.BlockSpec((tm, tn), lambda i, j: (i, j)),
        out_shape=jax.ShapeDtypeStruct((m, n), F32),
        compiler_params=_params(("parallel", "arbitrary")),
        name="matmul_residual",
    )(*lhs, *ws, res)


CONV_HALO = 16
SUBLANES = 8


def _conformer_kernel(vp_ref, gp_ref, v_ref, g_ref, vn_ref, gn_ref, dw_ref, db_ref, lg_ref, lb_ref,
                      o_ref, h_ref, *, tr, seq, width, sub):
    i = pl.program_id(0)
    half = width // 2
    has_prev = (i * tr) % seq != 0
    has_next = ((i + 1) * tr) % seq != 0
    rows = tr + 2 * CONV_HALO

    def glu(v, g):
        return v.astype(F32) * jax.nn.sigmoid(g.astype(F32))

    h_ref[0, 0:CONV_HALO, :] = jnp.where(has_prev, glu(vp_ref[...], gp_ref[...]), 0.0)
    h_ref[0, CONV_HALO:CONV_HALO + tr, :] = glu(v_ref[...], g_ref[...])
    h_ref[0, CONV_HALO + tr:, :] = jnp.where(has_next, glu(vn_ref[...], gn_ref[...]), 0.0)
    for b in range(1, SUBLANES):
        h_ref[b, 0:rows - SUBLANES, :] = h_ref[0, b:b + rows - SUBLANES, :]

    def body(r, carry):
        r0 = pl.multiple_of(r * sub, sub)
        acc = jnp.zeros((sub, h_ref.shape[2]), F32) + db_ref[...]
        for k in range(width):
            a, b = divmod(CONV_HALO - half + k, SUBLANES)
            acc = acc + dw_ref[k:k + 1, :] * h_ref[b, pl.ds(r0 + a * SUBLANES, sub), :]
        mu = jnp.mean(acc, axis=-1, keepdims=True)
        cen = acc - mu
        var = jnp.mean(cen * cen, axis=-1, keepdims=True)
        y = cen * lax.rsqrt(var + LN_EPS) * lg_ref[...] + lb_ref[...]
        o_ref[pl.ds(r0, sub), :] = (y * jax.nn.sigmoid(y)).astype(o_ref.dtype)
        return carry

    lax.fori_loop(0, tr // sub, body, 0)


def conformer_conv(z, dw, db, ln_g, ln_b, *, seq, tr):
    m = z.shape[0]
    width, c = dw.shape
    hb = tr // CONV_HALO
    last = m // CONV_HALO - 1
    prev_map = lambda col: (lambda i: (jnp.maximum(i * hb - 1, 0), col))
    next_map = lambda col: (lambda i: (jnp.minimum((i + 1) * hb, last), col))
    main_map = lambda col: (lambda i: (i, col))
    row = lambda a: a.reshape(1, c)
    full = lambda shape: pl.BlockSpec(shape, lambda i: (0, 0))
    return pl.pallas_call(
        functools.partial(_conformer_kernel, tr=tr, seq=seq, width=width, sub=32),
        grid=(m // tr,),
        in_specs=[pl.BlockSpec((CONV_HALO, c), prev_map(0)), pl.BlockSpec((CONV_HALO, c), prev_map(1)),
                  pl.BlockSpec((tr, c), main_map(0)), pl.BlockSpec((tr, c), main_map(1)),
                  pl.BlockSpec((CONV_HALO, c), next_map(0)), pl.BlockSpec((CONV_HALO, c), next_map(1)),
                  full((width, c)), full((1, c)), full((1, c)), full((1, c))],
        out_specs=pl.BlockSpec((tr, c), lambda i: (i, 0)),
        out_shape=jax.ShapeDtypeStruct((m, c), BF16),
        scratch_shapes=[pltpu.VMEM((SUBLANES, tr + 2 * CONV_HALO, c), F32)],
        compiler_params=_params(("parallel",)),
        name="conformer_conv",
    )(z, z, z, z, z, z, dw, row(db), row(ln_g), row(ln_b))


def _diff_attn_kernel(slopes_ref, q_ref, k_ref, v_ref, lq1_ref, lk1_ref, lq2_ref, lk2_ref, sg_ref, o_ref,
                      *, tq, dh, lambda_init):
    h = pl.program_id(1)
    qi = pl.program_id(2)
    s_len = k_ref.shape[0]
    lam = (jnp.exp(jnp.sum(lq1_ref[...] * lk1_ref[...], axis=-1, keepdims=True))
           - jnp.exp(jnp.sum(lq2_ref[...] * lk2_ref[...], axis=-1, keepdims=True)) + lambda_init)
    q = q_ref[...]
    k = k_ref[...]
    lane = lax.broadcasted_iota(jnp.int32, q.shape, 1)
    scale = jnp.asarray(dh ** -0.5, q.dtype)
    q1 = jnp.where(lane < dh, q, 0) * scale
    q2 = jnp.where(lane >= dh, q, 0) * scale
    qpos = qi * tq + lax.broadcasted_iota(jnp.int32, (tq, s_len), 0)
    kpos = lax.broadcasted_iota(jnp.int32, (tq, s_len), 1)
    bias = -slopes_ref[h] * jnp.abs(qpos - kpos).astype(F32)
    nt = (((1,), (1,)), ((), ()))

    def softmax_parts(qc):
        s = lax.dot_general(qc, k, nt, preferred_element_type=F32) + bias
        e = jnp.exp(s - jnp.max(s, axis=-1, keepdims=True))
        return e, jnp.sum(e, axis=-1, keepdims=True)

    e1, l1 = softmax_parts(q1)
    e2, l2 = softmax_parts(q2)
    w = e1 * (1.0 / l1) - e2 * (lam / l2)
    o = jnp.dot(w.astype(v_ref.dtype), v_ref[...], preferred_element_type=F32)
    o_ref[...] = (_rms(o, sg_ref[...], LN_EPS) * (1.0 - lambda_init)).astype(o_ref.dtype)


def diff_attention(z, lq1, lk1, lq2, lk2, subln_g, *, batch, seq, q_col, k_col, v_col, lambda_init, tq):
    dh = lq1.shape[-1]
    dv = subln_g.shape[-1]
    heads = (k_col - q_col) // (2 * dh)
    nq = seq // tq
    qb, kb, vb = q_col // dv, k_col // dv, v_col // dv
    slopes = jnp.exp2(-8.0 * jnp.arange(1, heads + 1, dtype=F32) / heads)
    vec = lambda a: a.reshape(1, -1)
    small = lambda n: pl.BlockSpec((1, n), lambda b, h, i: (0, 0))
    return pl.pallas_call(
        functools.partial(_diff_attn_kernel, tq=tq, dh=dh, lambda_init=lambda_init),
        grid=(batch, heads, nq),
        in_specs=[pl.BlockSpec(memory_space=pltpu.SMEM),
                  pl.BlockSpec((tq, 2 * dh), lambda b, h, i: (b * nq + i, qb + h)),
                  pl.BlockSpec((seq, 2 * dh), lambda b, h, i: (b, kb + h)),
                  pl.BlockSpec((seq, dv), lambda b, h, i: (b, vb + h)),
                  small(dh), small(dh), small(dh), small(dh), small(dv)],
        out_specs=pl.BlockSpec((tq, dv), lambda b, h, i: (b * nq + i, h)),
        out_shape=jax.ShapeDtypeStruct((batch * seq, heads * dv), BF16),
        compiler_params=_params(("parallel", "parallel", "arbitrary")),
        name="diff_attention",
    )(slopes, z, z, z, vec(lq1), vec(lk1), vec(lq2), vec(lk2), vec(subln_g))


def _cross_attn_kernel(q_ref, k_ref, v_ref, o_ref, *, heads):
    dh = q_ref.shape[1] // heads
    nt = (((1,), (1,)), ((), ()))
    for h in range(heads):
        cols = slice(h * dh, (h + 1) * dh)
        s = lax.dot_general(q_ref[:, cols], k_ref[:, cols], nt, preferred_element_type=F32) * (dh ** -0.5)
        e = jnp.exp(s - jnp.max(s, axis=-1, keepdims=True))
        p = e / jnp.sum(e, axis=-1, keepdims=True)
        o_ref[:, cols] = jnp.dot(p.astype(v_ref.dtype), v_ref[:, cols],
                                 preferred_element_type=F32).astype(o_ref.dtype)


def cross_attention(q, kv, *, batch, seq, mem_len, tq):
    d = q.shape[1]
    nq = seq // tq
    return pl.pallas_call(
        functools.partial(_cross_attn_kernel, heads=XA_HEADS),
        grid=(batch, nq),
        in_specs=[pl.BlockSpec((tq, d), lambda b, i: (b * nq + i, 0)),
                  pl.BlockSpec((mem_len, d), lambda b, i: (b, 0)),
                  pl.BlockSpec((mem_len, d), lambda b, i: (b, 1))],
        out_specs=pl.BlockSpec((tq, d), lambda b, i: (b * nq + i, 0)),
        out_shape=jax.ShapeDtypeStruct((batch * seq, d), BF16),
        compiler_params=_params(("parallel", "arbitrary")),
        name="cross_attention",
    )(q, kv, kv)


FFN_HALO = 8


def _ffn_kernel(xp_ref, x_ref, xn_ref, g_ref, wg_ref, wv_ref, dwg_ref, dwv_ref, dbg_ref, dbv_ref, wd_ref,
                fg_ref, o_ref, xs_ref, hg_ref, hv_ref, *, tm, seq, final_norm):
    i = pl.program_id(0)
    f = pl.program_id(1)

    @pl.when(f == 0)
    def _():
        has_prev = (i * tm) % seq != 0
        has_next = ((i + 1) * tm) % seq != 0
        norm = lambda x: _rms(x, g_ref[...], RMS_EPS)
        xs_ref[0:FFN_HALO, :] = jnp.where(has_prev, norm(xp_ref[...]), 0.0).astype(xs_ref.dtype)
        xs_ref[FFN_HALO:FFN_HALO + tm, :] = norm(x_ref[...]).astype(xs_ref.dtype)
        xs_ref[FFN_HALO + tm:, :] = jnp.where(has_next, norm(xn_ref[...]), 0.0).astype(xs_ref.dtype)
        o_ref[...] = x_ref[...]

    xs = xs_ref[...]
    hg_ref[...] = jnp.dot(xs, wg_ref[...], preferred_element_type=F32)
    hv_ref[...] = jnp.dot(xs, wv_ref[...], preferred_element_type=F32)

    def conv(h_ref, dw_ref, db_ref):
        out = db_ref[...] + dw_ref[0:1, :] * h_ref[FFN_HALO - 1:FFN_HALO - 1 + tm, :]
        out = out + dw_ref[1:2, :] * h_ref[FFN_HALO:FFN_HALO + tm, :]
        return out + dw_ref[2:3, :] * h_ref[FFN_HALO + 1:FFN_HALO + 1 + tm, :]

    gate = conv(hg_ref, dwg_ref, dbg_ref)
    act = gate * jax.nn.sigmoid(gate) * conv(hv_ref, dwv_ref, dbv_ref)
    o_ref[...] += jnp.dot(act.astype(wd_ref.dtype), wd_ref[...], preferred_element_type=F32)

    if final_norm:
        @pl.when(f == pl.num_programs(1) - 1)
        def _():
            o_ref[...] = _rms(o_ref[...], fg_ref[...], RMS_EPS)


def conv_ffn_block(x, g, w_up, dw, db, w_down, final_g, *, seq, tm, tf, final_norm):
    m, d = x.shape
    d_ff = w_down.shape[0]
    nf = d_ff // tf
    hb = tm // FFN_HALO
    last = m // FFN_HALO - 1
    db2 = db.reshape(1, 2 * d_ff)
    return pl.pallas_call(
        functools.partial(_ffn_kernel, tm=tm, seq=seq, final_norm=final_norm),
        grid=(m // tm, nf),
        in_specs=[pl.BlockSpec((FFN_HALO, d), lambda i, f: (jnp.maximum(i * hb - 1, 0), 0)),
                  pl.BlockSpec((tm, d), lambda i, f: (i, 0)),
                  pl.BlockSpec((FFN_HALO, d), lambda i, f: (jnp.minimum((i + 1) * hb, last), 0)),
                  pl.BlockSpec((1, d), lambda i, f: (0, 0)),
                  pl.BlockSpec((d, tf), lambda i, f: (0, f)),
                  pl.BlockSpec((d, tf), lambda i, f: (0, nf + f)),
                  pl.BlockSpec((dw.shape[0], tf), lambda i, f: (0, f)),
                  pl.BlockSpec((dw.shape[0], tf), lambda i, f: (0, nf + f)),
                  pl.BlockSpec((1, tf), lambda i, f: (0, f)),
                  pl.BlockSpec((1, tf), lambda i, f: (0, nf + f)),
                  pl.BlockSpec((tf, d), lambda i, f: (f, 0)),
                  pl.BlockSpec((1, d), lambda i, f: (0, 0))],
        out_specs=pl.BlockSpec((tm, d), lambda i, f: (i, 0)),
        out_shape=jax.ShapeDtypeStruct((m, d), F32),
        scratch_shapes=[pltpu.VMEM((tm + 2 * FFN_HALO, d), BF16),
                        pltpu.VMEM((tm + 2 * FFN_HALO, tf), F32),
                        pltpu.VMEM((tm + 2 * FFN_HALO, tf), F32)],
        compiler_params=_params(("parallel", "arbitrary")),
        name="conv_ffn",
    )(x, x, x, g.reshape(1, d), w_up, w_up, dw, dw, db2, db2, w_down, final_g.reshape(1, d))


def _s5_prep_kernel(lr_ref, li_ref, ldt_ref, btr_ref, bti_ref, ctr_ref, cti_ref, cr_ref, ci_ref,
                    vre_ref, vim_ref, kall_ref, wre_ref, wnim_ref, are_ref, aim_ref, *, chunk, ch):
    lr, li = lr_ref[...], li_ref[...]
    dt = jnp.exp(ldt_ref[...])
    mag = jnp.exp(lr * dt)
    lb_re, lb_im = mag * jnp.cos(li * dt), mag * jnp.sin(li * dt)
    den = lr * lr + li * li
    f_re = ((lb_re - 1.0) * lr + lb_im * li) / den
    f_im = (lb_im * lr - (lb_re - 1.0) * li) / den
    br, bi = btr_ref[...], bti_ref[...]
    bb_re = f_re * br - f_im * bi
    bb_im = f_re * bi + f_im * br
    lag = lax.broadcasted_iota(jnp.int32, br.shape, 3) // ch
    p_re, p_im = jnp.ones_like(lr), jnp.zeros_like(lr)
    pw_re, pw_im = jnp.zeros_like(br), jnp.zeros_like(br)
    pw1_re, pw1_im = jnp.zeros_like(br), jnp.zeros_like(br)
    for t in range(chunk + 1):
        if t < chunk:
            pw_re, pw_im = jnp.where(lag == t, p_re, pw_re), jnp.where(lag == t, p_im, pw_im)
        if t > 0:
            pw1_re, pw1_im = jnp.where(lag == t - 1, p_re, pw1_re), jnp.where(lag == t - 1, p_im, pw1_im)
        if t < chunk:
            p_re, p_im = p_re * lb_re - p_im * lb_im, p_re * lb_im + p_im * lb_re
    are_ref[...], aim_ref[...] = p_re, p_im
    v_re = pw_re * bb_re - pw_im * bb_im
    v_im = pw_re * bb_im + pw_im * bb_re
    vre_ref[...], vim_ref[...] = v_re, v_im
    ctr, cti = ctr_ref[...], cti_ref[...]
    wre_ref[...] = pw1_re * ctr - pw1_im * cti
    wnim_ref[...] = -(pw1_re * cti + pw1_im * ctr)
    gb = lr.shape[1]
    kdir = []
    for d in range(2):
        kd = (jnp.einsum("gop,gpn->gon", cr_ref[d], v_re[d], precision=lax.Precision.HIGHEST,
                         preferred_element_type=F32)
              - jnp.einsum("gop,gpn->gon", ci_ref[d], v_im[d], precision=lax.Precision.HIGHEST,
                           preferred_element_type=F32))
        kdir.append(kd)
    lag_k = lax.broadcasted_iota(jnp.int32, kdir[0].shape, 2) // ch
    kall_ref[0] = kdir[0] + jnp.where(lag_k == 0, kdir[1], 0.0)
    kall_ref[1] = kdir[1]
    del gb


def s5_operators(lam_re, lam_im, log_dt, b_re, b_im, c_re, c_im, *, gb):
    _, g, p = lam_re.shape
    ch = b_re.shape[-1]
    n = S5_CHUNK * ch
    col = lambda a: a.reshape(2, g, p, 1)
    tile = lambda a: jnp.tile(a, (1, 1, 1, S5_CHUNK))
    ct = lambda a: tile(jnp.swapaxes(a, 2, 3))
    spec = lambda *tail: pl.BlockSpec((2, gb) + tail, lambda i: (0, i, 0, 0))
    shp = lambda *tail: jax.ShapeDtypeStruct((2, g) + tail, F32)
    vre, vim, kall, wre, wnim, are, aim = pl.pallas_call(
        functools.partial(_s5_prep_kernel, chunk=S5_CHUNK, ch=ch),
        grid=(g // gb,),
        in_specs=[spec(p, 1), spec(p, 1), spec(1, 1), spec(p, n), spec(p, n), spec(p, n), spec(p, n),
                  spec(ch, p), spec(ch, p)],
        out_specs=[spec(p, n), spec(p, n), spec(ch, n), spec(p, n), spec(p, n), spec(p, 1), spec(p, 1)],
        out_shape=[shp(p, n), shp(p, n), shp(ch, n), shp(p, n), shp(p, n), shp(p, 1), shp(p, 1)],
        compiler_params=_params(("parallel",)),
        name="s5_prep",
    )(col(lam_re), col(lam_im), log_dt.reshape(2, g, 1, 1), tile(b_re), tile(b_im), ct(c_re), ct(c_im),
      c_re, c_im)

    L = S5_CHUNK
    s_idx = jnp.arange(L)[:, None]
    t_idx = jnp.arange(L)[None, :]
    k5 = kall.reshape(2, g, ch, L, ch)
    kf = jnp.take(k5[0], jnp.abs(t_idx - s_idx).reshape(-1), axis=2).reshape(g, ch, L, L, ch)
    kb = jnp.take(k5[1], jnp.abs(t_idx - s_idx).reshape(-1), axis=2).reshape(g, ch, L, L, ch)
    sel = (t_idx >= s_idx)[None, None, :, :, None]
    t_op = jnp.where(sel, kf, kb).transpose(0, 2, 4, 3, 1).reshape(g, n, n)

    v5 = lambda a: a.reshape(2, g, p, L, ch)
    e_f = lambda a: jnp.flip(v5(a)[0], axis=2).transpose(0, 2, 3, 1).reshape(g, n, p)
    e_b = lambda a: v5(a)[1].transpose(0, 2, 3, 1).reshape(g, n, p)
    e_op = jnp.concatenate([e_f(vre), e_b(vre), e_f(vim), e_b(vim)], axis=-1)

    w5 = lambda a: a.reshape(2, g, p, L, ch)
    f_f = lambda a: w5(a)[0].reshape(g, p, n)
    f_b = lambda a: jnp.flip(w5(a)[1], axis=2).reshape(g, p, n)
    f_op = jnp.concatenate([f_f(wre), f_b(wre), f_f(wnim), f_b(wnim)], axis=1)

    lanes = lambda a: jnp.concatenate([a[0, :, :, 0], a[1, :, :, 0]], axis=-1).reshape(g, 1, 2 * p)
    return t_op.astype(BF16), e_op.astype(BF16), f_op.astype(BF16), lanes(are), lanes(aim)


def _s5_group_kernel(x_ref, t_ref, e_ref, f_ref, are_ref, aim_ref, o_ref, s_ref, c_ref, *, nb, n_chunks):
    x = x_ref[0]
    half = e_ref.shape[2] // 2
    s_ref[...] = jnp.dot(x, e_ref[0], preferred_element_type=F32)
    a_re = jnp.broadcast_to(are_ref[0], (nb, half))
    a_im = jnp.broadcast_to(aim_ref[0], (nb, half))
    fwd = lax.broadcasted_iota(jnp.int32, (nb, half), 1) < half // 2

    def step(kk, carry):
        h_re, h_im = carry
        rf = pl.multiple_of(kk * nb, nb)
        rb = pl.multiple_of((n_chunks - 1 - kk) * nb, nb)
        c_ref[pl.ds(rf, nb), 0:half // 2] = h_re[:, 0:half // 2]
        c_ref[pl.ds(rb, nb), half // 2:half] = h_re[:, half // 2:half]
        c_ref[pl.ds(rf, nb), half:half + half // 2] = h_im[:, 0:half // 2]
        c_ref[pl.ds(rb, nb), half + half // 2:] = h_im[:, half // 2:half]
        s_re = jnp.where(fwd, s_ref[pl.ds(rf, nb), 0:half], s_ref[pl.ds(rb, nb), 0:half])
        s_im = jnp.where(fwd, s_ref[pl.ds(rf, nb), half:], s_ref[pl.ds(rb, nb), half:])
        return (a_re * h_re - a_im * h_im + s_re, a_re * h_im + a_im * h_re + s_im)

    zero = jnp.zeros((nb, half), F32)
    lax.fori_loop(0, n_chunks, step, (zero, zero))
    y = jnp.dot(x, t_ref[0], preferred_element_type=F32)
    y = y + jnp.dot(c_ref[...].astype(x.dtype), f_ref[0], preferred_element_type=F32)
    o_ref[0] = y.astype(o_ref.dtype)


def s5_scan(xg, t_op, e_op, f_op, a_re, a_im, *, nb):
    g, rows, n = xg.shape
    grp = lambda *tail: pl.BlockSpec((1,) + tail, lambda i: (i, 0, 0))
    return pl.pallas_call(
        functools.partial(_s5_group_kernel, nb=nb, n_chunks=rows // nb),
        grid=(g,),
        in_specs=[grp(rows, n), grp(n, n), grp(n, n), grp(n, n), grp(1, n // 2), grp(1, n // 2)],
        out_specs=grp(rows, n),
        out_shape=jax.ShapeDtypeStruct((g, rows, n), F32),
        scratch_shapes=[pltpu.VMEM((rows, n), F32), pltpu.VMEM((rows, n), F32)],
        compiler_params=_params(("parallel",)),
        name="s5_scan",
    )(xg, t_op, e_op, f_op, a_re, a_im)


def _s5_glu_kernel(x_ref, y_ref, g_ref, d_ref, wv_ref, wg_ref, o_ref, a_ref, *, tn):
    j = pl.program_id(1)

    @pl.when(j == 0)
    def _():
        u = y_ref[...] + d_ref[...] * _rms(x_ref[...], g_ref[...], RMS_EPS)
        act = 0.5 * u * (1.0 + jnp.tanh(math.sqrt(2.0 / math.pi) * (u + 0.044715 * (u * u * u))))
        a_ref[...] = act.astype(a_ref.dtype)

    a = a_ref[...]
    val = jnp.dot(a, wv_ref[...], preferred_element_type=F32)
    gate = jnp.dot(a, wg_ref[...], preferred_element_type=F32)
    c0 = pl.multiple_of(j * tn, tn)
    o_ref[...] = x_ref[:, pl.ds(c0, tn)] + val * jax.nn.sigmoid(gate)


def s5_glu(x, y, g, d_skip, w_val, w_gate, *, tm, tn):
    m, d = x.shape
    return pl.pallas_call(
        functools.partial(_s5_glu_kernel, tn=tn),
        grid=(m // tm, d // tn),
        in_specs=[pl.BlockSpec((tm, d), lambda i, j: (i, 0)),
                  pl.BlockSpec((tm, d), lambda i, j: (i, 0)),
                  pl.BlockSpec((1, d), lambda i, j: (0, 0)),
                  pl.BlockSpec((1, d), lambda i, j: (0, 0)),
                  pl.BlockSpec((d, tn), lambda i, j: (0, j)),
                  pl.BlockSpec((d, tn), lambda i, j: (0, j))],
        out_specs=pl.BlockSpec((tm, tn), lambda i, j: (i, j)),
        out_shape=jax.ShapeDtypeStruct((m, d), F32),
        scratch_shapes=[pltpu.VMEM((tm, d), BF16)],
        compiler_params=_params(("parallel", "arbitrary")),
        name="s5_glu",
    )(x, y, g.reshape(1, d), d_skip.reshape(1, d), w_val, w_gate)


def _mixer_conv_diff(x, norm_g, w_in, conv_dw, conv_db, conv_ln_g, conv_ln_b, lq1, lk1, lq2, lk2, subln_g, w_out,
                     *, batch, seq, lambda_init):
    d_conv = conv_dw.shape[1]
    d_diff = (w_in.shape[1] - 2 * d_conv) // 3
    z = norm_matmul(x, norm_g, w_in.astype(BF16), tm=1024, tn=1024)
    a_out = conformer_conv(z, conv_dw, conv_db, conv_ln_g, conv_ln_b, seq=seq, tr=256)
    b_out = diff_attention(z, lq1, lk1, lq2, lk2, subln_g, batch=batch, seq=seq,
                           q_col=2 * d_conv, k_col=2 * d_conv + d_diff, v_col=2 * d_conv + 2 * d_diff,
                           lambda_init=lambda_init, tq=512)
    w_out = w_out.astype(BF16)
    return matmul_residual([a_out, b_out], [w_out[:d_conv], w_out[d_conv:]], x, tm=1024, tn=1024)


def _mixer_s5(x, norm_g, lam_re, lam_im, log_dt, b_re, b_im, c_re, c_im, d_skip, w_val, w_gate, *, batch, seq):
    m, d = x.shape
    ch = b_re.shape[-1]
    groups = d // ch
    n_chunks = seq // S5_CHUNK
    t_op, e_op, f_op, a_re, a_im = s5_operators(lam_re, lam_im, log_dt, b_re, b_im, c_re, c_im, gb=8)
    xn = norm_rows(x, norm_g, tm=1024, dtype=BF16)
    xg = xn.reshape(batch, n_chunks, S5_CHUNK, groups, ch).transpose(3, 1, 0, 2, 4)
    xg = xg.reshape(groups, n_chunks * batch, S5_CHUNK * ch)
    yg = s5_scan(xg, t_op, e_op, f_op, a_re, a_im, nb=batch)
    y = yg.reshape(groups, n_chunks, batch, S5_CHUNK, ch).transpose(2, 1, 3, 0, 4).reshape(m, d)
    return s5_glu(x, y, norm_g, d_skip, w_val.astype(BF16), w_gate.astype(BF16), tm=512, tn=1024)


def _cross_attn_block(x, memf, norm_g, mem_g, wq, wk, wv, wo, *, batch, seq, mem_len):
    q = norm_matmul(x, norm_g, wq.astype(BF16), tm=1024, tn=1024)
    kv = norm_matmul(memf, mem_g, jnp.concatenate([wk, wv], axis=1).astype(BF16), tm=1024, tn=1024)
    o = cross_attention(q, kv, batch=batch, seq=seq, mem_len=mem_len, tq=512)
    return matmul_residual([o], [wo.astype(BF16)], x, tm=1024, tn=1024)


def kernel(x, mem, norm_mix_g, ab_w_in, conv_dw, conv_db, conv_ln_g, conv_ln_b, diff_lq1, diff_lk1, diff_lq2, diff_lk2, diff_subln_g, ab_w_out, s5_lam_re, s5_lam_im, s5_log_dt, s5_b_re, s5_b_im, s5_c_re, s5_c_im, s5_d, s5_w_val, s5_w_gate, norm_xa_g, norm_mem_g, xa_wq, xa_wk, xa_wv, xa_wo, norm_ffn_g, ffn_w_up, ffn_dw, ffn_db, ffn_w_down, final_g):
    batch, seq, d = x.shape
    mem_len = mem.shape[1]
    depth = norm_mix_g.shape[0]
    xf = x.reshape(batch * seq, d)
    memf = mem.reshape(batch * mem_len, d)
    for layer in range(depth):
        i = layer // 2
        if layer % 2 == 0:
            lambda_init = 0.8 - 0.6 * math.exp(-0.3 * layer)
            xf = _mixer_conv_diff(xf, norm_mix_g[layer], ab_w_in[i], conv_dw[i], conv_db[i], conv_ln_g[i],
                                  conv_ln_b[i], diff_lq1[i], diff_lk1[i], diff_lq2[i], diff_lk2[i],
                                  diff_subln_g[i], ab_w_out[i], batch=batch, seq=seq, lambda_init=lambda_init)
        else:
            xf = _mixer_s5(xf, norm_mix_g[layer], s5_lam_re[i], s5_lam_im[i], s5_log_dt[i], s5_b_re[i],
                           s5_b_im[i], s5_c_re[i], s5_c_im[i], s5_d[i], s5_w_val[i], s5_w_gate[i],
                           batch=batch, seq=seq)
        xf = _cross_attn_block(xf, memf, norm_xa_g[layer], norm_mem_g[layer], xa_wq[layer], xa_wk[layer],
                               xa_wv[layer], xa_wo[layer], batch=batch, seq=seq, mem_len=mem_len)
        xf = conv_ffn_block(xf, norm_ffn_g[layer], ffn_w_up[layer].astype(BF16), ffn_dw[layer], ffn_db[layer],
                            ffn_w_down[layer].astype(BF16), final_g, seq=seq, tm=512, tf=512,
                            final_norm=(layer == depth - 1))
    return xf.reshape(batch, seq, d)
```

```python
import functools
import math

import jax
import jax.numpy as jnp
from jax import lax
from jax.experimental import pallas as pl
from jax.experimental.pallas import tpu as pltpu

RMS_EPS = 1e-6
LN_EPS = 1e-5
XA_HEADS = 4
S5_CHUNK = 16
V7X_VMEM_BUDGET = 56 * 1024 * 1024
BF16 = jnp.bfloat16
F32 = jnp.float32


def _params(semantics, vmem=V7X_VMEM_BUDGET):
    return pltpu.CompilerParams(dimension_semantics=semantics, vmem_limit_bytes=vmem)


def _rms(x, g, eps):
    return x * lax.rsqrt(jnp.mean(x * x, axis=-1, keepdims=True) + eps) * g


def _norm_matmul_kernel(x_ref, g_ref, w_ref, o_ref, xn_ref):
    @pl.when(pl.program_id(1) == 0)
    def _():
        xn_ref[...] = _rms(x_ref[...], g_ref[...], RMS_EPS).astype(xn_ref.dtype)

    o_ref[...] = jnp.dot(xn_ref[...], w_ref[...], preferred_element_type=F32).astype(o_ref.dtype)


def norm_matmul(x, g, w, *, tm, tn):
    m, k = x.shape
    n = w.shape[1]
    return pl.pallas_call(
        _norm_matmul_kernel,
        grid=(m // tm, n // tn),
        in_specs=[pl.BlockSpec((tm, k), lambda i, j: (i, 0)),
                  pl.BlockSpec((1, k), lambda i, j: (0, 0)),
                  pl.BlockSpec((k, tn), lambda i, j: (0, j))],
        out_specs=pl.BlockSpec((tm, tn), lambda i, j: (i, j)),
        out_shape=jax.ShapeDtypeStruct((m, n), BF16),
        scratch_shapes=[pltpu.VMEM((tm, k), BF16)],
        compiler_params=_params(("parallel", "arbitrary")),
        name="norm_matmul",
    )(x, g.reshape(1, k), w)


def _norm_kernel(x_ref, g_ref, o_ref):
    o_ref[...] = _rms(x_ref[...], g_ref[...], RMS_EPS).astype(o_ref.dtype)


def norm_rows(x, g, *, tm, dtype):
    m, k = x.shape
    return pl.pallas_call(
        _norm_kernel,
        grid=(m // tm,),
        in_specs=[pl.BlockSpec((tm, k), lambda i: (i, 0)), pl.BlockSpec((1, k), lambda i: (0, 0))],
        out_specs=pl.BlockSpec((tm, k), lambda i: (i, 0)),
        out_shape=jax.ShapeDtypeStruct((m, k), dtype),
        compiler_params=_params(("parallel",)),
        name="norm_rows",
    )(x, g.reshape(1, k))


def _matmul_residual_kernel(*refs, n_lhs):
    a_refs, w_refs = refs[:n_lhs], refs[n_lhs:2 * n_lhs]
    r_ref, o_ref = refs[2 * n_lhs], refs[2 * n_lhs + 1]
    acc = r_ref[...]
    for a_ref, w_ref in zip(a_refs, w_refs):
        acc = acc + jnp.dot(a_ref[...], w_ref[...], preferred_element_type=F32)
    o_ref[...] = acc


def matmul_residual(lhs, ws, res, *, tm, tn):
    m, n = res.shape
    in_specs = [pl.BlockSpec((tm, a.shape[1]), lambda i, j: (i, 0)) for a in lhs]
    in_specs += [pl.BlockSpec((w.shape[0], tn), lambda i, j: (0, j)) for w in ws]
    in_specs += [pl.BlockSpec((tm, tn), lambda i, j: (i, j))]
    return pl.pallas_call(
        functools.partial(_matmul_residual_kernel, n_lhs=len(lhs)),
        grid=(m // tm, n // tn),
        in_specs=in_specs,
        out_specs=pl.BlockSpec((tm, tn), lambda i, j: (i, j)),
        out_shape=jax.ShapeDtypeStruct((m, n), F32),
        compiler_params=_params(("parallel", "arbitrary")),
        name="matmul_residual",
    )(*lhs, *ws, res)


CONV_HALO = 16
SUBLANES = 8


def _conformer_kernel(vp_ref, gp_ref, v_ref, g_ref, vn_ref, gn_ref, dw_ref, db_ref, lg_ref, lb_ref,
                      o_ref, h_ref, *, tr, seq, width, sub):
    i = pl.program_id(0)
    half = width // 2
    has_prev = (i * tr) % seq != 0
    has_next = ((i + 1) * tr) % seq != 0
    rows = tr + 2 * CONV_HALO

    def glu(v, g):
        return v.astype(F32) * jax.nn.sigmoid(g.astype(F32))

    h_ref[0, 0:CONV_HALO, :] = jnp.where(has_prev, glu(vp_ref[...], gp_ref[...]), 0.0)
    h_ref[0, CONV_HALO:CONV_HALO + tr, :] = glu(v_ref[...], g_ref[...])
    h_ref[0, CONV_HALO + tr:, :] = jnp.where(has_next, glu(vn_ref[...], gn_ref[...]), 0.0)
    for b in range(1, SUBLANES):
        h_ref[b, 0:rows - SUBLANES, :] = h_ref[0, b:b + rows - SUBLANES, :]

    def body(r, carry):
        r0 = pl.multiple_of(r * sub, sub)
        acc = jnp.zeros((sub, h_ref.shape[2]), F32) + db_ref[...]
        for k in range(width):
            a, b = divmod(CONV_HALO - half + k, SUBLANES)
            acc = acc + dw_ref[k:k + 1, :] * h_ref[b, pl.ds(r0 + a * SUBLANES, sub), :]
        mu = jnp.mean(acc, axis=-1, keepdims=True)
        cen = acc - mu
        var = jnp.mean(cen * cen, axis=-1, keepdims=True)
        y = cen * lax.rsqrt(var + LN_EPS) * lg_ref[...] + lb_ref[...]
        o_ref[pl.ds(r0, sub), :] = (y * jax.nn.sigmoid(y)).astype(o_ref.dtype)
        return carry

    lax.fori_loop(0, tr // sub, body, 0)


def conformer_conv(z, dw, db, ln_g, ln_b, *, seq, tr):
    m = z.shape[0]
    width, c = dw.shape
    hb = tr // CONV_HALO
    last = m // CONV_HALO - 1
    prev_map = lambda col: (lambda i: (jnp.maximum(i * hb - 1, 0), col))
    next_map = lambda col: (lambda i: (jnp.minimum((i + 1) * hb, last), col))
    main_map = lambda col: (lambda i: (i, col))
    row = lambda a: a.reshape(1, c)
    full = lambda shape: pl.BlockSpec(shape, lambda i: (0, 0))
    return pl.pallas_call(
        functools.partial(_conformer_kernel, tr=tr, seq=seq, width=width, sub=32),
        grid=(m // tr,),
        in_specs=[pl.BlockSpec((CONV_HALO, c), prev_map(0)), pl.BlockSpec((CONV_HALO, c), prev_map(1)),
                  pl.BlockSpec((tr, c), main_map(0)), pl.BlockSpec((tr, c), main_map(1)),
                  pl.BlockSpec((CONV_HALO, c), next_map(0)), pl.BlockSpec((CONV_HALO, c), next_map(1)),
                  full((width, c)), full((1, c)), full((1, c)), full((1, c))],
        out_specs=pl.BlockSpec((tr, c), lambda i: (i, 0)),
        out_shape=jax.ShapeDtypeStruct((m, c), BF16),
        scratch_shapes=[pltpu.VMEM((SUBLANES, tr + 2 * CONV_HALO, c), F32)],
        compiler_params=_params(("parallel",)),
        name="conformer_conv",
    )(z, z, z, z, z, z, dw, row(db), row(ln_g), row(ln_b))


def _diff_attn_kernel(slopes_ref, q_ref, k_ref, v_ref, lq1_ref, lk1_ref, lq2_ref, lk2_ref, sg_ref, o_ref,
                      *, tq, dh, lambda_init):
    h = pl.program_id(1)
    qi = pl.program_id(2)
    s_len = k_ref.shape[0]
    lam = (jnp.exp(jnp.sum(lq1_ref[...] * lk1_ref[...], axis=-1, keepdims=True))
           - jnp.exp(jnp.sum(lq2_ref[...] * lk2_ref[...], axis=-1, keepdims=True)) + lambda_init)
    q = q_ref[...]
    k = k_ref[...]
    lane = lax.broadcasted_iota(jnp.int32, q.shape, 1)
    scale = jnp.asarray(dh ** -0.5, q.dtype)
    q1 = jnp.where(lane < dh, q, 0) * scale
    q2 = jnp.where(lane >= dh, q, 0) * scale
    qpos = qi * tq + lax.broadcasted_iota(jnp.int32, (tq, s_len), 0)
    kpos = lax.broadcasted_iota(jnp.int32, (tq, s_len), 1)
    bias = -slopes_ref[h] * jnp.abs(qpos - kpos).astype(F32)
    nt = (((1,), (1,)), ((), ()))

    def softmax_parts(qc):
        s = lax.dot_general(qc, k, nt, preferred_element_type=F32) + bias
        e = jnp.exp(s - jnp.max(s, axis=-1, keepdims=True))
        return e, jnp.sum(e, axis=-1, keepdims=True)

    e1, l1 = softmax_parts(q1)
    e2, l2 = softmax_parts(q2)
    w = e1 * (1.0 / l1) - e2 * (lam / l2)
    o = jnp.dot(w.astype(v_ref.dtype), v_ref[...], preferred_element_type=F32)
    o_ref[...] = (_rms(o, sg_ref[...], LN_EPS) * (1.0 - lambda_init)).astype(o_ref.dtype)


def diff_attention(z, lq1, lk1, lq2, lk2, subln_g, *, batch, seq, q_col, k_col, v_col, lambda_init, tq):
    dh = lq1.shape[-1]
    dv = subln_g.shape[-1]
    heads = (k_col - q_col) // (2 * dh)
    nq = seq // tq
    qb, kb, vb = q_col // dv, k_col // dv, v_col // dv
    slopes = jnp.exp2(-8.0 * jnp.arange(1, heads + 1, dtype=F32) / heads)
    vec = lambda a: a.reshape(1, -1)
    small = lambda n: pl.BlockSpec((1, n), lambda b, h, i: (0, 0))
    return pl.pallas_call(
        functools.partial(_diff_attn_kernel, tq=tq, dh=dh, lambda_init=lambda_init),
        grid=(batch, heads, nq),
        in_specs=[pl.BlockSpec(memory_space=pltpu.SMEM),
                  pl.BlockSpec((tq, 2 * dh), lambda b, h, i: (b * nq + i, qb + h)),
                  pl.BlockSpec((seq, 2 * dh), lambda b, h, i: (b, kb + h)),
                  pl.BlockSpec((seq, dv), lambda b, h, i: (b, vb + h)),
                  small(dh), small(dh), small(dh), small(dh), small(dv)],
        out_specs=pl.BlockSpec((tq, dv), lambda b, h, i: (b * nq + i, h)),
        out_shape=jax.ShapeDtypeStruct((batch * seq, heads * dv), BF16),
        compiler_params=_params(("parallel", "parallel", "arbitrary")),
        name="diff_attention",
    )(slopes, z, z, z, vec(lq1), vec(lk1), vec(lq2), vec(lk2), vec(subln_g))


def _cross_attn_kernel(q_ref, k_ref, v_ref, o_ref, *, heads):
    dh = q_ref.shape[1] // heads
    nt = (((1,), (1,)), ((), ()))
    for h in range(heads):
        cols = slice(h * dh, (h + 1) * dh)
        s = lax.dot_general(q_ref[:, cols], k_ref[:, cols], nt, preferred_element_type=F32) * (dh ** -0.5)
        e = jnp.exp(s - jnp.max(s, axis=-1, keepdims=True))
        p = e / jnp.sum(e, axis=-1, keepdims=True)
        o_ref[:, cols] = jnp.dot(p.astype(v_ref.dtype), v_ref[:, cols],
                                 preferred_element_type=F32).astype(o_ref.dtype)


def cross_attention(q, kv, *, batch, seq, mem_len, tq):
    d = q.shape[1]
    nq = seq // tq
    return pl.pallas_call(
        functools.partial(_cross_attn_kernel, heads=XA_HEADS),
        grid=(batch, nq),
        in_specs=[pl.BlockSpec((tq, d), lambda b, i: (b * nq + i, 0)),
                  pl.BlockSpec((mem_len, d), lambda b, i: (b, 0)),
                  pl.BlockSpec((mem_len, d), lambda b, i: (b, 1))],
        out_specs=pl.BlockSpec((tq, d), lambda b, i: (b * nq + i, 0)),
        out_shape=jax.ShapeDtypeStruct((batch * seq, d), BF16),
        compiler_params=_params(("parallel", "arbitrary")),
        name="cross_attention",
    )(q, kv, kv)


FFN_HALO = 8


def _ffn_kernel(xp_ref, x_ref, xn_ref, g_ref, wg_ref, wv_ref, dwg_ref, dwv_ref, dbg_ref, dbv_ref, wd_ref,
                fg_ref, o_ref, xs_ref, hg_ref, hv_ref, *, tm, seq, final_norm):
    i = pl.program_id(0)
    f = pl.program_id(1)

    @pl.when(f == 0)
    def _():
        has_prev = (i * tm) % seq != 0
        has_next = ((i + 1) * tm) % seq != 0
        norm = lambda x: _rms(x, g_ref[...], RMS_EPS)
        xs_ref[0:FFN_HALO, :] = jnp.where(has_prev, norm(xp_ref[...]), 0.0).astype(xs_ref.dtype)
        xs_ref[FFN_HALO:FFN_HALO + tm, :] = norm(x_ref[...]).astype(xs_ref.dtype)
        xs_ref[FFN_HALO + tm:, :] = jnp.where(has_next, norm(xn_ref[...]), 0.0).astype(xs_ref.dtype)
        o_ref[...] = x_ref[...]

    xs = xs_ref[...]
    hg_ref[...] = jnp.dot(xs, wg_ref[...], preferred_element_type=F32)
    hv_ref[...] = jnp.dot(xs, wv_ref[...], preferred_element_type=F32)

    def conv(h_ref, dw_ref, db_ref):
        out = db_ref[...] + dw_ref[0:1, :] * h_ref[FFN_HALO - 1:FFN_HALO - 1 + tm, :]
        out = out + dw_ref[1:2, :] * h_ref[FFN_HALO:FFN_HALO + tm, :]
        return out + dw_ref[2:3, :] * h_ref[FFN_HALO + 1:FFN_HALO + 1 + tm, :]

    gate = conv(hg_ref, dwg_ref, dbg_ref)
    act = gate * jax.nn.sigmoid(gate) * conv(hv_ref, dwv_ref, dbv_ref)
    o_ref[...] += jnp.dot(act.astype(wd_ref.dtype), wd_ref[...], preferred_element_type=F32)

    if final_norm:
        @pl.when(f == pl.num_programs(1) - 1)
        def _():
            o_ref[...] = _rms(o_ref[...], fg_ref[...], RMS_EPS)


def conv_ffn_block(x, g, w_up, dw, db, w_down, final_g, *, seq, tm, tf, final_norm):
    m, d = x.shape
    d_ff = w_down.shape[0]
    nf = d_ff // tf
    hb = tm // FFN_HALO
    last = m // FFN_HALO - 1
    db2 = db.reshape(1, 2 * d_ff)
    return pl.pallas_call(
        functools.partial(_ffn_kernel, tm=tm, seq=seq, final_norm=final_norm),
        grid=(m // tm, nf),
        in_specs=[pl.BlockSpec((FFN_HALO, d), lambda i, f: (jnp.maximum(i * hb - 1, 0), 0)),
                  pl.BlockSpec((tm, d), lambda i, f: (i, 0)),
                  pl.BlockSpec((FFN_HALO, d), lambda i, f: (jnp.minimum((i + 1) * hb, last), 0)),
                  pl.BlockSpec((1, d), lambda i, f: (0, 0)),
                  pl.BlockSpec((d, tf), lambda i, f: (0, f)),
                  pl.BlockSpec((d, tf), lambda i, f: (0, nf + f)),
                  pl.BlockSpec((dw.shape[0], tf), lambda i, f: (0, f)),
                  pl.BlockSpec((dw.shape[0], tf), lambda i, f: (0, nf + f)),
                  pl.BlockSpec((1, tf), lambda i, f: (0, f)),
                  pl.BlockSpec((1, tf), lambda i, f: (0, nf + f)),
                  pl.BlockSpec((tf, d), lambda i, f: (f, 0)),
                  pl.BlockSpec((1, d), lambda i, f: (0, 0))],
        out_specs=pl.BlockSpec((tm, d), lambda i, f: (i, 0)),
        out_shape=jax.ShapeDtypeStruct((m, d), F32),
        scratch_shapes=[pltpu.VMEM((tm + 2 * FFN_HALO, d), BF16),
                        pltpu.VMEM((tm + 2 * FFN_HALO, tf), F32),
                        pltpu.VMEM((tm + 2 * FFN_HALO, tf), F32)],
        compiler_params=_params(("parallel", "arbitrary")),
        name="conv_ffn",
    )(x, x, x, g.reshape(1, d), w_up, w_up, dw, dw, db2, db2, w_down, final_g.reshape(1, d))


def _s5_prep_kernel(lr_ref, li_ref, ldt_ref, btr_ref, bti_ref, ctr_ref, cti_ref, cr_ref, ci_ref,
                    vre_ref, vim_ref, kall_ref, wre_ref, wnim_ref, are_ref, aim_ref, *, chunk, ch):
    lr, li = lr_ref[...], li_ref[...]
    dt = jnp.exp(ldt_ref[...])
    mag = jnp.exp(lr * dt)
    lb_re, lb_im = mag * jnp.cos(li * dt), mag * jnp.sin(li * dt)
    den = lr * lr + li * li
    f_re = ((lb_re - 1.0) * lr + lb_im * li) / den
    f_im = (lb_im * lr - (lb_re - 1.0) * li) / den
    br, bi = btr_ref[...], bti_ref[...]
    bb_re = f_re * br - f_im * bi
    bb_im = f_re * bi + f_im * br
    lag = lax.broadcasted_iota(jnp.int32, br.shape, 3) // ch
    p_re, p_im = jnp.ones_like(lr), jnp.zeros_like(lr)
    pw_re, pw_im = jnp.zeros_like(br), jnp.zeros_like(br)
    pw1_re, pw1_im = jnp.zeros_like(br), jnp.zeros_like(br)
    for t in range(chunk + 1):
        if t < chunk:
            pw_re, pw_im = jnp.where(lag == t, p_re, pw_re), jnp.where(lag == t, p_im, pw_im)
        if t > 0:
            pw1_re, pw1_im = jnp.where(lag == t - 1, p_re, pw1_re), jnp.where(lag == t - 1, p_im, pw1_im)
        if t < chunk:
            p_re, p_im = p_re * lb_re - p_im * lb_im, p_re * lb_im + p_im * lb_re
    are_ref[...], aim_ref[...] = p_re, p_im
    v_re = pw_re * bb_re - pw_im * bb_im
    v_im = pw_re * bb_im + pw_im * bb_re
    vre_ref[...], vim_ref[...] = v_re, v_im
    ctr, cti = ctr_ref[...], cti_ref[...]
    wre_ref[...] = pw1_re * ctr - pw1_im * cti
    wnim_ref[...] = -(pw1_re * cti + pw1_im * ctr)
    gb = lr.shape[1]
    kdir = []
    for d in range(2):
        kd = (jnp.einsum("gop,gpn->gon", cr_ref[d], v_re[d], precision=lax.Precision.HIGHEST,
                         preferred_element_type=F32)
              - jnp.einsum("gop,gpn->gon", ci_ref[d], v_im[d], precision=lax.Precision.HIGHEST,
                           preferred_element_type=F32))
        kdir.append(kd)
    lag_k = lax.broadcasted_iota(jnp.int32, kdir[0].shape, 2) // ch
    kall_ref[0] = kdir[0] + jnp.where(lag_k == 0, kdir[1], 0.0)
    kall_ref[1] = kdir[1]
    del gb


def s5_operators(lam_re, lam_im, log_dt, b_re, b_im, c_re, c_im, *, gb):
    _, g, p = lam_re.shape
    ch = b_re.shape[-1]
    n = S5_CHUNK * ch
    col = lambda a: a.reshape(2, g, p, 1)
    tile = lambda a: jnp.tile(a, (1, 1, 1, S5_CHUNK))
    ct = lambda a: tile(jnp.swapaxes(a, 2, 3))
    spec = lambda *tail: pl.BlockSpec((2, gb) + tail, lambda i: (0, i, 0, 0))
    shp = lambda *tail: jax.ShapeDtypeStruct((2, g) + tail, F32)
    vre, vim, kall, wre, wnim, are, aim = pl.pallas_call(
        functools.partial(_s5_prep_kernel, chunk=S5_CHUNK, ch=ch),
        grid=(g // gb,),
        in_specs=[spec(p, 1), spec(p, 1), spec(1, 1), spec(p, n), spec(p, n), spec(p, n), spec(p, n),
                  spec(ch, p), spec(ch, p)],
        out_specs=[spec(p, n), spec(p, n), spec(ch, n), spec(p, n), spec(p, n), spec(p, 1), spec(p, 1)],
        out_shape=[shp(p, n), shp(p, n), shp(ch, n), shp(p, n), shp(p, n), shp(p, 1), shp(p, 1)],
        compiler_params=_params(("parallel",)),
        name="s5_prep",
    )(col(lam_re), col(lam_im), log_dt.reshape(2, g, 1, 1), tile(b_re), tile(b_im), ct(c_re), ct(c_im),
      c_re, c_im)

    L = S5_CHUNK
    s_idx = jnp.arange(L)[:, None]
    t_idx = jnp.arange(L)[None, :]
    k5 = kall.reshape(2, g, ch, L, ch)
    kf = jnp.take(k5[0], jnp.abs(t_idx - s_idx).reshape(-1), axis=2).reshape(g, ch, L, L, ch)
    kb = jnp.take(k5[1], jnp.abs(t_idx - s_idx).reshape(-1), axis=2).reshape(g, ch, L, L, ch)
    sel = (t_idx >= s_idx)[None, None, :, :, None]
    t_op = jnp.where(sel, kf, kb).transpose(0, 2, 4, 3, 1).reshape(g, n, n)

    v5 = lambda a: a.reshape(2, g, p, L, ch)
    e_f = lambda a: jnp.flip(v5(a)[0], axis=2).transpose(0, 2, 3, 1).reshape(g, n, p)
    e_b = lambda a: v5(a)[1].transpose(0, 2, 3, 1).reshape(g, n, p)
    e_op = jnp.concatenate([e_f(vre), e_b(vre), e_f(vim), e_b(vim)], axis=-1)

    w5 = lambda a: a.reshape(2, g, p, L, ch)
    f_f = lambda a: w5(a)[0].reshape(g, p, n)
    f_b = lambda a: jnp.flip(w5(a)[1], axis=2).reshape(g, p, n)
    f_op = jnp.concatenate([f_f(wre), f_b(wre), f_f(wnim), f_b(wnim)], axis=1)

    lanes = lambda a: jnp.concatenate([a[0, :, :, 0], a[1, :, :, 0]], axis=-1).reshape(g, 1, 2 * p)
    return t_op.astype(BF16), e_op.astype(BF16), f_op.astype(BF16), lanes(are), lanes(aim)


def _s5_group_kernel(x_ref, t_ref, e_ref, f_ref, are_ref, aim_ref, o_ref, s_ref, c_ref, *, nb, n_chunks):
    x = x_ref[0]
    half = e_ref.shape[2] // 2
    s_ref[...] = jnp.dot(x, e_ref[0], preferred_element_type=F32)
    a_re = jnp.broadcast_to(are_ref[0], (nb, half))
    a_im = jnp.broadcast_to(aim_ref[0], (nb, half))
    fwd = lax.broadcasted_iota(jnp.int32, (nb, half), 1) < half // 2

    def step(kk, carry):
        h_re, h_im = carry
        rf = pl.multiple_of(kk * nb, nb)
        rb = pl.multiple_of((n_chunks - 1 - kk) * nb, nb)
        c_ref[pl.ds(rf, nb), 0:half // 2] = h_re[:, 0:half // 2]
        c_ref[pl.ds(rb, nb), half // 2:half] = h_re[:, half // 2:half]
        c_ref[pl.ds(rf, nb), half:half + half // 2] = h_im[:, 0:half // 2]
        c_ref[pl.ds(rb, nb), half + half // 2:] = h_im[:, half // 2:half]
        s_re = jnp.where(fwd, s_ref[pl.ds(rf, nb), 0:half], s_ref[pl.ds(rb, nb), 0:half])
        s_im = jnp.where(fwd, s_ref[pl.ds(rf, nb), half:], s_ref[pl.ds(rb, nb), half:])
        return (a_re * h_re - a_im * h_im + s_re, a_re * h_im + a_im * h_re + s_im)

    zero = jnp.zeros((nb, half), F32)
    lax.fori_loop(0, n_chunks, step, (zero, zero))
    y = jnp.dot(x, t_ref[0], preferred_element_type=F32)
    y = y + jnp.dot(c_ref[...].astype(x.dtype), f_ref[0], preferred_element_type=F32)
    o_ref[0] = y.astype(o_ref.dtype)


def s5_scan(xg, t_op, e_op, f_op, a_re, a_im, *, nb):
    g, rows, n = xg.shape
    grp = lambda *tail: pl.BlockSpec((1,) + tail, lambda i: (i, 0, 0))
    return pl.pallas_call(
        functools.partial(_s5_group_kernel, nb=nb, n_chunks=rows // nb),
        grid=(g,),
        in_specs=[grp(rows, n), grp(n, n), grp(n, n), grp(n, n), grp(1, n // 2), grp(1, n // 2)],
        out_specs=grp(rows, n),
        out_shape=jax.ShapeDtypeStruct((g, rows, n), F32),
        scratch_shapes=[pltpu.VMEM((rows, n), F32), pltpu.VMEM((rows, n), F32)],
        compiler_params=_params(("parallel",)),
        name="s5_scan",
    )(xg, t_op, e_op, f_op, a_re, a_im)


LANES = 128


def _piece_transpose(v, piece, width):
    n = len(v)
    v = list(v)
    d = 1
    while d < n:
        low = (piece & d) == 0
        for j in range(n):
            if j & d == 0:
                a, b = v[j], v[j + d]
                v[j] = jnp.where(low, a, pltpu.roll(b, d * width, 1))
                v[j + d] = jnp.where(low, pltpu.roll(a, LANES - d * width, 1), b)
        d *= 2
    return v


def _inv_rms_to(inv_ref, x_ref):
    x = x_ref[...]
    inv = lax.rsqrt(jnp.mean(x * x, axis=-1, keepdims=True) + RMS_EPS)
    inv_ref[...] = jnp.broadcast_to(inv, inv_ref.shape)


def _s5_in_kernel(x_ref, g_ref, o_ref, inv_ref, *, tb, ch):
    nb, _, d = x_ref.shape
    per = LANES // ch
    _inv_rms_to(inv_ref, x_ref)
    piece = lax.broadcasted_iota(jnp.int32, (nb, LANES), 1) // ch

    def tile_body(lt, carry):
        l0 = pl.multiple_of(lt * LANES, LANES)
        gl = g_ref[:, pl.ds(l0, LANES)]
        for c in range(tb):
            for h in range(S5_CHUNK // per):
                r0 = c * S5_CHUNK + h * per
                v = [x_ref[:, r0 + s, pl.ds(l0, LANES)] * inv_ref[:, r0 + s, :] * gl for s in range(per)]
                w = _piece_transpose(v, piece, ch)
                for gg in range(per):
                    o_ref[lt * per + gg, c * nb:(c + 1) * nb, h * LANES:(h + 1) * LANES] = w[gg].astype(o_ref.dtype)
        return carry

    lax.fori_loop(0, d // LANES, tile_body, 0)


def s5_to_groups(x3, g, *, ch, tb):
    nb, s, d = x3.shape
    rows = tb * S5_CHUNK
    return pl.pallas_call(
        functools.partial(_s5_in_kernel, tb=tb, ch=ch),
        grid=(s // rows,),
        in_specs=[pl.BlockSpec((nb, rows, d), lambda i: (0, i, 0)), pl.BlockSpec((1, d), lambda i: (0, 0))],
        out_specs=pl.BlockSpec((d // ch, tb * nb, S5_CHUNK * ch), lambda i: (0, i, 0)),
        out_shape=jax.ShapeDtypeStruct((d // ch, (s // S5_CHUNK) * nb, S5_CHUNK * ch), BF16),
        scratch_shapes=[pltpu.VMEM((nb, rows, LANES), F32)],
        compiler_params=_params(("parallel",)),
        name="s5_to_groups",
    )(x3, g.reshape(1, d))


def _s5_out_kernel(yg_ref, x_ref, g_ref, d_ref, o_ref, inv_ref, *, tb, ch):
    nb, _, d = x_ref.shape
    per = LANES // ch
    _inv_rms_to(inv_ref, x_ref)
    piece = lax.broadcasted_iota(jnp.int32, (nb, LANES), 1) // ch

    def tile_body(lt, carry):
        l0 = pl.multiple_of(lt * LANES, LANES)
        gl = g_ref[:, pl.ds(l0, LANES)]
        dl = d_ref[:, pl.ds(l0, LANES)]
        for c in range(tb):
            for h in range(S5_CHUNK // per):
                r0 = c * S5_CHUNK + h * per
                w = [yg_ref[lt * per + gg, c * nb:(c + 1) * nb, h * LANES:(h + 1) * LANES] for gg in range(per)]
                v = _piece_transpose(w, piece, ch)
                for s in range(per):
                    xn = x_ref[:, r0 + s, pl.ds(l0, LANES)] * inv_ref[:, r0 + s, :] * gl
                    u = v[s] + dl * xn
                    act = 0.5 * u * (1.0 + jnp.tanh(math.sqrt(2.0 / math.pi) * (u + 0.044715 * (u * u * u))))
                    o_ref[:, r0 + s, pl.ds(l0, LANES)] = act
        return carry

    lax.fori_loop(0, d // LANES, tile_body, 0)


def s5_from_groups(yg, x3, g, d_skip, *, ch, tb):
    nb, s, d = x3.shape
    rows = tb * S5_CHUNK
    return pl.pallas_call(
        functools.partial(_s5_out_kernel, tb=tb, ch=ch),
        grid=(s // rows,),
        in_specs=[pl.BlockSpec((d // ch, tb * nb, S5_CHUNK * ch), lambda i: (0, i, 0)),
                  pl.BlockSpec((nb, rows, d), lambda i: (0, i, 0)),
                  pl.BlockSpec((1, d), lambda i: (0, 0)),
                  pl.BlockSpec((1, d), lambda i: (0, 0))],
        out_specs=pl.BlockSpec((nb, rows, d), lambda i: (0, i, 0)),
        out_shape=jax.ShapeDtypeStruct((nb, s, d), F32),
        scratch_shapes=[pltpu.VMEM((nb, rows, LANES), F32)],
        compiler_params=_params(("parallel",)),
        name="s5_from_groups",
    )(yg, x3, g.reshape(1, d), d_skip.reshape(1, d))


def _s5_glu_kernel(a_ref, x_ref, wv_ref, wg_ref, o_ref, ab_ref):
    @pl.when(pl.program_id(1) == 0)
    def _():
        ab_ref[...] = a_ref[...].astype(ab_ref.dtype)

    a = ab_ref[...]
    val = jnp.dot(a, wv_ref[...], preferred_element_type=F32)
    gate = jnp.dot(a, wg_ref[...], preferred_element_type=F32)
    o_ref[...] = x_ref[...] + val * jax.nn.sigmoid(gate)


def s5_glu(act, x, w_val, w_gate, *, tm, tn):
    m, d = x.shape
    return pl.pallas_call(
        _s5_glu_kernel,
        grid=(m // tm, d // tn),
        in_specs=[pl.BlockSpec((tm, d), lambda i, j: (i, 0)),
                  pl.BlockSpec((tm, tn), lambda i, j: (i, j)),
                  pl.BlockSpec((d, tn), lambda i, j: (0, j)),
                  pl.BlockSpec((d, tn), lambda i, j: (0, j))],
        out_specs=pl.BlockSpec((tm, tn), lambda i, j: (i, j)),
        out_shape=jax.ShapeDtypeStruct((m, d), F32),
        scratch_shapes=[pltpu.VMEM((tm, d), BF16)],
        compiler_params=_params(("parallel", "arbitrary")),
        name="s5_glu",
    )(act, x, w_val, w_gate)


def _mixer_conv_diff(x, norm_g, w_in, conv_dw, conv_db, conv_ln_g, conv_ln_b, lq1, lk1, lq2, lk2, subln_g, w_out,
                     *, batch, seq, lambda_init):
    d_conv = conv_dw.shape[1]
    d_diff = (w_in.shape[1] - 2 * d_conv) // 3
    z = norm_matmul(x, norm_g, w_in.astype(BF16), tm=1024, tn=1024)
    a_out = conformer_conv(z, conv_dw, conv_db, conv_ln_g, conv_ln_b, seq=seq, tr=256)
    b_out = diff_attention(z, lq1, lk1, lq2, lk2, subln_g, batch=batch, seq=seq,
                           q_col=2 * d_conv, k_col=2 * d_conv + d_diff, v_col=2 * d_conv + 2 * d_diff,
                           lambda_init=lambda_init, tq=512)
    w_out = w_out.astype(BF16)
    return matmul_residual([a_out, b_out], [w_out[:d_conv], w_out[d_conv:]], x, tm=1024, tn=1024)


def _mixer_s5(x, norm_g, lam_re, lam_im, log_dt, b_re, b_im, c_re, c_im, d_skip, w_val, w_gate, *, batch, seq):
    m, d = x.shape
    ch = b_re.shape[-1]
    t_op, e_op, f_op, a_re, a_im = s5_operators(lam_re, lam_im, log_dt, b_re, b_im, c_re, c_im, gb=8)
    x3 = x.reshape(batch, seq, d)
    xg = s5_to_groups(x3, norm_g, ch=ch, tb=4)
    yg = s5_scan(xg, t_op, e_op, f_op, a_re, a_im, nb=batch)
    act = s5_from_groups(yg, x3, norm_g, d_skip, ch=ch, tb=4).reshape(m, d)
    return s5_glu(act, x, w_val.astype(BF16), w_gate.astype(BF16), tm=512, tn=1024)


def _cross_attn_block(x, memf, norm_g, mem_g, wq, wk, wv, wo, *, batch, seq, mem_len):
    q = norm_matmul(x, norm_g, wq.astype(BF16), tm=1024, tn=1024)
    kv = norm_matmul(memf, mem_g, jnp.concatenate([wk, wv], axis=1).astype(BF16), tm=1024, tn=1024)
    o = cross_attention(q, kv, batch=batch, seq=seq, mem_len=mem_len, tq=512)
    return matmul_residual([o], [wo.astype(BF16)], x, tm=1024, tn=1024)


def kernel(x, mem, norm_mix_g, ab_w_in, conv_dw, conv_db, conv_ln_g, conv_ln_b, diff_lq1, diff_lk1, diff_lq2, diff_lk2, diff_subln_g, ab_w_out, s5_lam_re, s5_lam_im, s5_log_dt, s5_b_re, s5_b_im, s5_c_re, s5_c_im, s5_d, s5_w_val, s5_w_gate, norm_xa_g, norm_mem_g, xa_wq, xa_wk, xa_wv, xa_wo, norm_ffn_g, ffn_w_up, ffn_dw, ffn_db, ffn_w_down, final_g):
    batch, seq, d = x.shape
    mem_len = mem.shape[1]
    depth = norm_mix_g.shape[0]
    xf = x.reshape(batch * seq, d)
    memf = mem.reshape(batch * mem_len, d)
    for layer in range(depth):
        i = layer // 2
        if layer % 2 == 0:
            lambda_init = 0.8 - 0.6 * math.exp(-0.3 * layer)
            xf = _mixer_conv_diff(xf, norm_mix_g[layer], ab_w_in[i], conv_dw[i], conv_db[i], conv_ln_g[i],
                                  conv_ln_b[i], diff_lq1[i], diff_lk1[i], diff_lq2[i], diff_lk2[i],
                                  diff_subln_g[i], ab_w_out[i], batch=batch, seq=seq, lambda_init=lambda_init)
        else:
            xf = _mixer_s5(xf, norm_mix_g[layer], s5_lam_re[i], s5_lam_im[i], s5_log_dt[i], s5_b_re[i],
                           s5_b_im[i], s5_c_re[i], s5_c_im[i], s5_d[i], s5_w_val[i], s5_w_gate[i],
                           batch=batch, seq=seq)
        xf = _cross_attn_block(xf, memf, norm_xa_g[layer], norm_mem_g[layer], xa_wq[layer], xa_wk[layer],
                               xa_wv[layer], xa_wo[layer], batch=batch, seq=seq, mem_len=mem_len)
        xf = conv_ffn_block(xf, norm_ffn_g[layer], ffn_w_up[layer].astype(BF16), ffn_dw[layer], ffn_db[layer],
                            ffn_w_down[layer].astype(BF16), final_g, seq=seq, tm=512, tf=512,
                            final_norm=(layer == depth - 1))
    return xf.reshape(batch, seq, d)
```

```python
import functools
import math

import jax
import jax.numpy as jnp
from jax import lax
from jax.experimental import pallas as pl
from jax.experimental.pallas import tpu as pltpu

RMS_EPS = 1e-6
LN_EPS = 1e-5
XA_HEADS = 4
S5_CHUNK = 16
V7X_VMEM_BUDGET = 56 * 1024 * 1024
BF16 = jnp.bfloat16
F32 = jnp.float32


def _params(semantics, vmem=V7X_VMEM_BUDGET):
    return pltpu.CompilerParams(dimension_semantics=semantics, vmem_limit_bytes=vmem)


def _resident(shape):
    return pl.BlockSpec(shape, lambda *_: (0,) * len(shape), pipeline_mode=pl.Buffered(1))


def _rms(x, g, eps):
    return x * lax.rsqrt(jnp.mean(x * x, axis=-1, keepdims=True) + eps) * g


def _norm_matmul_kernel(x_ref, g_ref, w_ref, o_ref, xn_ref):
    @pl.when(pl.program_id(1) == 0)
    def _():
        xn_ref[...] = _rms(x_ref[...], g_ref[...], RMS_EPS).astype(xn_ref.dtype)

    o_ref[...] = jnp.dot(xn_ref[...], w_ref[...], preferred_element_type=F32).astype(o_ref.dtype)


def norm_matmul(x, g, w, *, tm, tn):
    m, k = x.shape
    n = w.shape[1]
    w_spec = _resident((k, n)) if tn == n else pl.BlockSpec((k, tn), lambda i, j: (0, j))
    return pl.pallas_call(
        _norm_matmul_kernel,
        grid=(m // tm, n // tn),
        in_specs=[pl.BlockSpec((tm, k), lambda i, j: (i, 0)),
                  pl.BlockSpec((1, k), lambda i, j: (0, 0)),
                  w_spec],
        out_specs=pl.BlockSpec((tm, tn), lambda i, j: (i, j)),
        out_shape=jax.ShapeDtypeStruct((m, n), BF16),
        scratch_shapes=[pltpu.VMEM((tm, k), BF16)],
        compiler_params=_params(("parallel", "arbitrary")),
        name="norm_matmul",
    )(x, g.reshape(1, k), w)


def _matmul_residual_kernel(*refs, n_lhs):
    a_refs, w_refs = refs[:n_lhs], refs[n_lhs:2 * n_lhs]
    r_ref, o_ref = refs[2 * n_lhs], refs[2 * n_lhs + 1]
    acc = r_ref[...]
    for a_ref, w_ref in zip(a_refs, w_refs):
        acc = acc + jnp.dot(a_ref[...], w_ref[...], preferred_element_type=F32)
    o_ref[...] = acc


def matmul_residual(lhs, w, res, *, tm):
    m, n = res.shape
    kb = w.shape[0] // len(lhs)
    assert all(a.shape[1] == kb for a in lhs)
    in_specs = [pl.BlockSpec((tm, kb), lambda i: (i, 0)) for _ in lhs]
    in_specs += [pl.BlockSpec((kb, n), lambda i, r=r: (r, 0), pipeline_mode=pl.Buffered(1)) for r in range(len(lhs))]
    in_specs += [pl.BlockSpec((tm, n), lambda i: (i, 0))]
    return pl.pallas_call(
        functools.partial(_matmul_residual_kernel, n_lhs=len(lhs)),
        grid=(m // tm,),
        in_specs=in_specs,
        out_specs=pl.BlockSpec((tm, n), lambda i: (i, 0)),
        out_shape=jax.ShapeDtypeStruct((m, n), F32),
        compiler_params=_params(("parallel",)),
        name="matmul_residual",
    )(*lhs, *([w] * len(lhs)), res)


CONV_HALO = 16
SUBLANES = 8


def _conformer_kernel(vp_ref, gp_ref, v_ref, g_ref, vn_ref, gn_ref, dw_ref, db_ref, lg_ref, lb_ref,
                      o_ref, h_ref, *, tr, seq, width, sub):
    i = pl.program_id(0)
    half = width // 2
    has_prev = (i * tr) % seq != 0
    has_next = ((i + 1) * tr) % seq != 0
    rows = tr + 2 * CONV_HALO

    def glu(v, g):
        return v.astype(F32) * jax.nn.sigmoid(g.astype(F32))

    h_ref[0, 0:CONV_HALO, :] = jnp.where(has_prev, glu(vp_ref[...], gp_ref[...]), 0.0)
    h_ref[0, CONV_HALO:CONV_HALO + tr, :] = glu(v_ref[...], g_ref[...])
    h_ref[0, CONV_HALO + tr:, :] = jnp.where(has_next, glu(vn_ref[...], gn_ref[...]), 0.0)
    for b in range(1, SUBLANES):
        h_ref[b, 0:rows - SUBLANES, :] = h_ref[0, b:b + rows - SUBLANES, :]

    def body(r, carry):
        r0 = pl.multiple_of(r * sub, sub)
        acc = jnp.zeros((sub, h_ref.shape[2]), F32) + db_ref[...]
        for k in range(width):
            a, b = divmod(CONV_HALO - half + k, SUBLANES)
            acc = acc + dw_ref[k:k + 1, :] * h_ref[b, pl.ds(r0 + a * SUBLANES, sub), :]
        mu = jnp.mean(acc, axis=-1, keepdims=True)
        cen = acc - mu
        var = jnp.mean(cen * cen, axis=-1, keepdims=True)
        y = cen * lax.rsqrt(var + LN_EPS) * lg_ref[...] + lb_ref[...]
        o_ref[pl.ds(r0, sub), :] = (y * jax.nn.sigmoid(y)).astype(o_ref.dtype)
        return carry

    lax.fori_loop(0, tr // sub, body, 0, unroll=2)


def conformer_conv(z, dw, db, ln_g, ln_b, *, seq, tr):
    m = z.shape[0]
    width, c = dw.shape
    hb = tr // CONV_HALO
    last = m // CONV_HALO - 1
    prev_map = lambda col: (lambda i: (jnp.maximum(i * hb - 1, 0), col))
    next_map = lambda col: (lambda i: (jnp.minimum((i + 1) * hb, last), col))
    main_map = lambda col: (lambda i: (i, col))
    row = lambda a: a.reshape(1, c)
    full = lambda shape: pl.BlockSpec(shape, lambda i: (0, 0))
    return pl.pallas_call(
        functools.partial(_conformer_kernel, tr=tr, seq=seq, width=width, sub=32),
        grid=(m // tr,),
        in_specs=[pl.BlockSpec((CONV_HALO, c), prev_map(0)), pl.BlockSpec((CONV_HALO, c), prev_map(1)),
                  pl.BlockSpec((tr, c), main_map(0)), pl.BlockSpec((tr, c), main_map(1)),
                  pl.BlockSpec((CONV_HALO, c), next_map(0)), pl.BlockSpec((CONV_HALO, c), next_map(1)),
                  full((width, c)), full((1, c)), full((1, c)), full((1, c))],
        out_specs=pl.BlockSpec((tr, c), lambda i: (i, 0)),
        out_shape=jax.ShapeDtypeStruct((m, c), BF16),
        scratch_shapes=[pltpu.VMEM((SUBLANES, tr + 2 * CONV_HALO, c), F32)],
        compiler_params=_params(("parallel",)),
        name="conformer_conv",
    )(z, z, z, z, z, z, dw, row(db), row(ln_g), row(ln_b))


def _diff_attn_kernel(slopes_ref, q_ref, k_ref, v_ref, lq1_ref, lk1_ref, lq2_ref, lk2_ref, sg_ref, o_ref, bias_ref,
                      *, tq, sub, dh, lambda_init):
    h = pl.program_id(0)
    qi = pl.program_id(2)
    s_len = k_ref.shape[0]

    @pl.when((pl.program_id(1) == 0) & (qi == 0))
    def _():
        row = lax.broadcasted_iota(jnp.int32, bias_ref.shape, 0)
        col = lax.broadcasted_iota(jnp.int32, bias_ref.shape, 1)
        bias_ref[...] = -slopes_ref[h] * jnp.abs(row + (s_len - tq) - col).astype(F32)

    lam = (jnp.exp(jnp.sum(lq1_ref[...] * lk1_ref[...], axis=-1, keepdims=True))
           - jnp.exp(jnp.sum(lq2_ref[...] * lk2_ref[...], axis=-1, keepdims=True)) + lambda_init)
    k = k_ref[...]
    start = pl.multiple_of((s_len - tq) - qi * tq, tq)
    scale = jnp.asarray(dh ** -0.5, q_ref.dtype)
    nt = (((1,), (1,)), ((), ()))

    def scores(r0):
        q = q_ref[r0:r0 + sub, :]
        lane = lax.broadcasted_iota(jnp.int32, q.shape, 1)
        return [lax.dot_general(jnp.where(m, q, 0) * scale, k, nt, preferred_element_type=F32)
                for m in (lane < dh, lane >= dh)]

    def softmax_parts(s, bias):
        s = s + bias
        e = jnp.exp(s - jnp.max(s, axis=-1, keepdims=True))
        return e, jnp.sum(e, axis=-1, keepdims=True)

    starts = list(range(0, tq, sub))
    s_next = scores(starts[0])
    for n, r0 in enumerate(starts):
        s1, s2 = s_next
        if n + 1 < len(starts):
            s_next = scores(starts[n + 1])
        bias = bias_ref[r0:r0 + sub, pl.ds(start, s_len)]
        e1, l1 = softmax_parts(s1, bias)
        e2, l2 = softmax_parts(s2, bias)
        w = e1 * (1.0 / l1) - e2 * (lam / l2)
        o = jnp.dot(w.astype(v_ref.dtype), v_ref[...], preferred_element_type=F32)
        o_ref[r0:r0 + sub, :] = (_rms(o, sg_ref[...], LN_EPS) * (1.0 - lambda_init)).astype(o_ref.dtype)


def diff_attention(z, lq1, lk1, lq2, lk2, subln_g, *, batch, seq, q_col, k_col, v_col, lambda_init, tq):
    dh = lq1.shape[-1]
    dv = subln_g.shape[-1]
    heads = (k_col - q_col) // (2 * dh)
    nq = seq // tq
    qb, kb, vb = q_col // dv, k_col // dv, v_col // dv
    slopes = jnp.exp2(-8.0 * jnp.arange(1, heads + 1, dtype=F32) / heads)
    vec = lambda a: a.reshape(1, -1)
    small = lambda n: pl.BlockSpec((1, n), lambda h, b, i: (0, 0))
    return pl.pallas_call(
        functools.partial(_diff_attn_kernel, tq=tq, sub=128, dh=dh, lambda_init=lambda_init),
        grid=(heads, batch, nq),
        in_specs=[pl.BlockSpec(memory_space=pltpu.SMEM),
                  pl.BlockSpec((tq, 2 * dh), lambda h, b, i: (b * nq + i, qb + h)),
                  pl.BlockSpec((seq, 2 * dh), lambda h, b, i: (b, kb + h)),
                  pl.BlockSpec((seq, dv), lambda h, b, i: (b, vb + h)),
                  small(dh), small(dh), small(dh), small(dh), small(dv)],
        out_specs=pl.BlockSpec((tq, dv), lambda h, b, i: (b * nq + i, h)),
        out_shape=jax.ShapeDtypeStruct((batch * seq, heads * dv), BF16),
        scratch_shapes=[pltpu.VMEM((tq, 2 * seq - tq), F32)],
        compiler_params=_params(("arbitrary", "arbitrary", "arbitrary")),
        name="diff_attention",
    )(slopes, z, z, z, vec(lq1), vec(lk1), vec(lq2), vec(lk2), vec(subln_g))


def _cross_attn_kernel(q_ref, k_ref, v_ref, o_ref, *, heads):
    dh = q_ref.shape[1] // heads
    nt = (((1,), (1,)), ((), ()))
    for h in range(heads):
        cols = slice(h * dh, (h + 1) * dh)
        s = lax.dot_general(q_ref[:, cols], k_ref[:, cols], nt, preferred_element_type=F32) * (dh ** -0.5)
        e = jnp.exp(s - jnp.max(s, axis=-1, keepdims=True))
        p = e / jnp.sum(e, axis=-1, keepdims=True)
        o_ref[:, cols] = jnp.dot(p.astype(v_ref.dtype), v_ref[:, cols],
                                 preferred_element_type=F32).astype(o_ref.dtype)


def cross_attention(q, kv, *, batch, seq, mem_len, tq):
    d = q.shape[1]
    nq = seq // tq
    return pl.pallas_call(
        functools.partial(_cross_attn_kernel, heads=XA_HEADS),
        grid=(batch, nq),
        in_specs=[pl.BlockSpec((tq, d), lambda b, i: (b * nq + i, 0)),
                  pl.BlockSpec((mem_len, d), lambda b, i: (b, 0)),
                  pl.BlockSpec((mem_len, d), lambda b, i: (b, 1))],
        out_specs=pl.BlockSpec((tq, d), lambda b, i: (b * nq + i, 0)),
        out_shape=jax.ShapeDtypeStruct((batch * seq, d), BF16),
        compiler_params=_params(("parallel", "arbitrary")),
        name="cross_attention",
    )(q, kv, kv)


FFN_HALO = 8


def _ffn_kernel(xp_ref, x_ref, xn_ref, g_ref, wg_ref, wv_ref, dwg_ref, dwv_ref, dbg_ref, dbv_ref, wd_ref,
                fg_ref, o_ref, xs_ref, hg_ref, hv_ref, *, tm, seq, final_norm):
    i = pl.program_id(0)
    f = pl.program_id(1)

    @pl.when(f == 0)
    def _():
        has_prev = (i * tm) % seq != 0
        has_next = ((i + 1) * tm) % seq != 0
        norm = lambda x: _rms(x, g_ref[...], RMS_EPS)
        xs_ref[0:FFN_HALO, :] = jnp.where(has_prev, norm(xp_ref[...]), 0.0).astype(xs_ref.dtype)
        xs_ref[FFN_HALO:FFN_HALO + tm, :] = norm(x_ref[...]).astype(xs_ref.dtype)
        xs_ref[FFN_HALO + tm:, :] = jnp.where(has_next, norm(xn_ref[...]), 0.0).astype(xs_ref.dtype)
        o_ref[...] = x_ref[...]

    xs = xs_ref[...]
    hg_ref[...] = jnp.dot(xs, wg_ref[...], preferred_element_type=F32)
    hv_ref[...] = jnp.dot(xs, wv_ref[...], preferred_element_type=F32)

    def conv(h_ref, dw_ref, db_ref):
        out = db_ref[...] + dw_ref[0:1, :] * h_ref[FFN_HALO - 1:FFN_HALO - 1 + tm, :]
        out = out + dw_ref[1:2, :] * h_ref[FFN_HALO:FFN_HALO + tm, :]
        return out + dw_ref[2:3, :] * h_ref[FFN_HALO + 1:FFN_HALO + 1 + tm, :]

    gate = conv(hg_ref, dwg_ref, dbg_ref)
    act = gate * jax.nn.sigmoid(gate) * conv(hv_ref, dwv_ref, dbv_ref)
    o_ref[...] += jnp.dot(act.astype(wd_ref.dtype), wd_ref[...], preferred_element_type=F32)

    if final_norm:
        @pl.when(f == pl.num_programs(1) - 1)
        def _():
            o_ref[...] = _rms(o_ref[...], fg_ref[...], RMS_EPS)


def conv_ffn_block(x, g, w_up, dw, db, w_down, final_g, *, seq, tm, tf, final_norm):
    m, d = x.shape
    d_ff = w_down.shape[0]
    nf = d_ff // tf
    hb = tm // FFN_HALO
    last = m // FFN_HALO - 1
    db2 = db.reshape(1, 2 * d_ff)
    return pl.pallas_call(
        functools.partial(_ffn_kernel, tm=tm, seq=seq, final_norm=final_norm),
        grid=(m // tm, nf),
        in_specs=[pl.BlockSpec((FFN_HALO, d), lambda i, f: (jnp.maximum(i * hb - 1, 0), 0)),
                  pl.BlockSpec((tm, d), lambda i, f: (i, 0)),
                  pl.BlockSpec((FFN_HALO, d), lambda i, f: (jnp.minimum((i + 1) * hb, last), 0)),
                  pl.BlockSpec((1, d), lambda i, f: (0, 0)),
                  pl.BlockSpec((d, tf), lambda i, f: (0, f)),
                  pl.BlockSpec((d, tf), lambda i, f: (0, nf + f)),
                  pl.BlockSpec((dw.shape[0], tf), lambda i, f: (0, f)),
                  pl.BlockSpec((dw.shape[0], tf), lambda i, f: (0, nf + f)),
                  pl.BlockSpec((1, tf), lambda i, f: (0, f)),
                  pl.BlockSpec((1, tf), lambda i, f: (0, nf + f)),
                  pl.BlockSpec((tf, d), lambda i, f: (f, 0)),
                  pl.BlockSpec((1, d), lambda i, f: (0, 0))],
        out_specs=pl.BlockSpec((tm, d), lambda i, f: (i, 0)),
        out_shape=jax.ShapeDtypeStruct((m, d), F32),
        scratch_shapes=[pltpu.VMEM((tm + 2 * FFN_HALO, d), BF16),
                        pltpu.VMEM((tm + 2 * FFN_HALO, tf), F32),
                        pltpu.VMEM((tm + 2 * FFN_HALO, tf), F32)],
        compiler_params=_params(("parallel", "arbitrary")),
        name="conv_ffn",
    )(x, x, x, g.reshape(1, d), w_up, w_up, dw, dw, db2, db2, w_down, final_g.reshape(1, d))


def _s5_prep_kernel(lr_ref, li_ref, ldt_ref, btr_ref, bti_ref, ctr_ref, cti_ref, cr_ref, ci_ref,
                    vre_ref, vim_ref, kall_ref, wre_ref, wnim_ref, are_ref, aim_ref, *, chunk, ch):
    lr, li = lr_ref[...], li_ref[...]
    dt = jnp.exp(ldt_ref[...])
    mag = jnp.exp(lr * dt)
    lb_re, lb_im = mag * jnp.cos(li * dt), mag * jnp.sin(li * dt)
    den = lr * lr + li * li
    f_re = ((lb_re - 1.0) * lr + lb_im * li) / den
    f_im = (lb_im * lr - (lb_re - 1.0) * li) / den
    br, bi = btr_ref[...], bti_ref[...]
    bb_re = f_re * br - f_im * bi
    bb_im = f_re * bi + f_im * br
    lag = lax.broadcasted_iota(jnp.int32, br.shape, 3) // ch
    p_re, p_im = jnp.ones_like(lr), jnp.zeros_like(lr)
    pw_re, pw_im = jnp.zeros_like(br), jnp.zeros_like(br)
    pw1_re, pw1_im = jnp.zeros_like(br), jnp.zeros_like(br)
    for t in range(chunk + 1):
        if t < chunk:
            pw_re, pw_im = jnp.where(lag == t, p_re, pw_re), jnp.where(lag == t, p_im, pw_im)
        if t > 0:
            pw1_re, pw1_im = jnp.where(lag == t - 1, p_re, pw1_re), jnp.where(lag == t - 1, p_im, pw1_im)
        if t < chunk:
            p_re, p_im = p_re * lb_re - p_im * lb_im, p_re * lb_im + p_im * lb_re
    are_ref[...], aim_ref[...] = p_re, p_im
    v_re = pw_re * bb_re - pw_im * bb_im
    v_im = pw_re * bb_im + pw_im * bb_re
    vre_ref[...], vim_ref[...] = v_re, v_im
    ctr, cti = ctr_ref[...], cti_ref[...]
    wre_ref[...] = pw1_re * ctr - pw1_im * cti
    wnim_ref[...] = -(pw1_re * cti + pw1_im * ctr)
    gb = lr.shape[1]
    kdir = []
    for d in range(2):
        kd = (jnp.einsum("gop,gpn->gon", cr_ref[d], v_re[d], precision=lax.Precision.HIGHEST,
                         preferred_element_type=F32)
              - jnp.einsum("gop,gpn->gon", ci_ref[d], v_im[d], precision=lax.Precision.HIGHEST,
                           preferred_element_type=F32))
        kdir.append(kd)
    lag_k = lax.broadcasted_iota(jnp.int32, kdir[0].shape, 2) // ch
    kall_ref[0] = kdir[0] + jnp.where(lag_k == 0, kdir[1], 0.0)
    kall_ref[1] = kdir[1]
    del gb


def s5_operators(lam_re, lam_im, log_dt, b_re, b_im, c_re, c_im, *, gb):
    _, g, p = lam_re.shape
    ch = b_re.shape[-1]
    n = S5_CHUNK * ch
    col = lambda a: a.reshape(2, g, p, 1)
    tile = lambda a: jnp.tile(a, (1, 1, 1, S5_CHUNK))
    ct = lambda a: tile(jnp.swapaxes(a, 2, 3))
    spec = lambda *tail: pl.BlockSpec((2, gb) + tail, lambda i: (0, i, 0, 0))
    shp = lambda *tail: jax.ShapeDtypeStruct((2, g) + tail, F32)
    vre, vim, kall, wre, wnim, are, aim = pl.pallas_call(
        functools.partial(_s5_prep_kernel, chunk=S5_CHUNK, ch=ch),
        grid=(g // gb,),
        in_specs=[spec(p, 1), spec(p, 1), spec(1, 1), spec(p, n), spec(p, n), spec(p, n), spec(p, n),
                  spec(ch, p), spec(ch, p)],
        out_specs=[spec(p, n), spec(p, n), spec(ch, n), spec(p, n), spec(p, n), spec(p, 1), spec(p, 1)],
        out_shape=[shp(p, n), shp(p, n), shp(ch, n), shp(p, n), shp(p, n), shp(p, 1), shp(p, 1)],
        compiler_params=_params(("parallel",)),
        name="s5_prep",
    )(col(lam_re), col(lam_im), log_dt.reshape(2, g, 1, 1), tile(b_re), tile(b_im), ct(c_re), ct(c_im),
      c_re, c_im)

    L = S5_CHUNK
    s_idx = jnp.arange(L)[:, None]
    t_idx = jnp.arange(L)[None, :]
    k5 = kall.reshape(2, g, ch, L, ch)
    kf = jnp.take(k5[0], jnp.abs(t_idx - s_idx).reshape(-1), axis=2).reshape(g, ch, L, L, ch)
    kb = jnp.take(k5[1], jnp.abs(t_idx - s_idx).reshape(-1), axis=2).reshape(g, ch, L, L, ch)
    sel = (t_idx >= s_idx)[None, None, :, :, None]
    t_op = jnp.where(sel, kf, kb).transpose(0, 2, 4, 3, 1).reshape(g, n, n)

    v5 = lambda a: a.reshape(2, g, p, L, ch)
    e_f = lambda a: jnp.flip(v5(a)[0], axis=2).transpose(0, 2, 3, 1).reshape(g, n, p)
    e_b = lambda a: v5(a)[1].transpose(0, 2, 3, 1).reshape(g, n, p)
    e_op = jnp.concatenate([e_f(vre), e_b(vre), e_f(vim), e_b(vim)], axis=-1)

    w5 = lambda a: a.reshape(2, g, p, L, ch)
    f_f = lambda a: w5(a)[0].reshape(g, p, n)
    f_b = lambda a: jnp.flip(w5(a)[1], axis=2).reshape(g, p, n)
    f_op = jnp.concatenate([f_f(wre), f_b(wre), f_f(wnim), f_b(wnim)], axis=1)

    lanes = lambda a: jnp.concatenate([a[0, :, :, 0], a[1, :, :, 0]], axis=-1).reshape(g, 1, 2 * p)
    return t_op.astype(BF16), e_op.astype(BF16), f_op.astype(BF16), lanes(are), lanes(aim)


def _s5_group_kernel(x_ref, t_ref, e_ref, f_ref, are_ref, aim_ref, o_ref, s_ref, c_ref, *, nb, n_chunks):
    x = x_ref[0]
    half = e_ref.shape[2] // 2
    s_ref[...] = jnp.dot(x, e_ref[0], preferred_element_type=F32)
    a_re = jnp.broadcast_to(are_ref[0], (nb, half))
    a_im = jnp.broadcast_to(aim_ref[0], (nb, half))
    fwd = lax.broadcasted_iota(jnp.int32, (nb, half), 1) < half // 2

    def step(kk, carry):
        h_re, h_im = carry
        rf = pl.multiple_of(kk * nb, nb)
        rb = pl.multiple_of((n_chunks - 1 - kk) * nb, nb)
        c_ref[pl.ds(rf, nb), 0:half // 2] = h_re[:, 0:half // 2]
        c_ref[pl.ds(rb, nb), half // 2:half] = h_re[:, half // 2:half]
        c_ref[pl.ds(rf, nb), half:half + half // 2] = h_im[:, 0:half // 2]
        c_ref[pl.ds(rb, nb), half + half // 2:] = h_im[:, half // 2:half]
        s_re = jnp.where(fwd, s_ref[pl.ds(rf, nb), 0:half], s_ref[pl.ds(rb, nb), 0:half])
        s_im = jnp.where(fwd, s_ref[pl.ds(rf, nb), half:], s_ref[pl.ds(rb, nb), half:])
        return (a_re * h_re - a_im * h_im + s_re, a_re * h_im + a_im * h_re + s_im)

    zero = jnp.zeros((nb, half), F32)
    lax.fori_loop(0, n_chunks, step, (zero, zero))
    y = jnp.dot(x, t_ref[0], preferred_element_type=F32)
    y = y + jnp.dot(c_ref[...].astype(x.dtype), f_ref[0], preferred_element_type=F32)
    o_ref[0] = y.astype(o_ref.dtype)


def s5_scan(xg, t_op, e_op, f_op, a_re, a_im, *, nb):
    g, rows, n = xg.shape
    grp = lambda *tail: pl.BlockSpec((1,) + tail, lambda i: (i, 0, 0))
    return pl.pallas_call(
        functools.partial(_s5_group_kernel, nb=nb, n_chunks=rows // nb),
        grid=(g,),
        in_specs=[grp(rows, n), grp(n, n), grp(n, n), grp(n, n), grp(1, n // 2), grp(1, n // 2)],
        out_specs=grp(rows, n),
        out_shape=jax.ShapeDtypeStruct((g, rows, n), F32),
        scratch_shapes=[pltpu.VMEM((rows, n), F32), pltpu.VMEM((rows, n), F32)],
        compiler_params=_params(("parallel",)),
        name="s5_scan",
    )(xg, t_op, e_op, f_op, a_re, a_im)


LANES = 128


def _grid_transpose(v, idx, axis, unit):
    n = len(v)
    size = n * unit
    v = list(v)
    d = 1
    while d < n:
        low = (idx & d) == 0
        for j in range(n):
            if j & d == 0:
                a, b = v[j], v[j + d]
                v[j] = jnp.where(low, a, pltpu.roll(b, d * unit, axis))
                v[j + d] = jnp.where(low, pltpu.roll(a, size - d * unit, axis), b)
        d *= 2
    return v


def _inv_rms_to(inv_ref, x_ref):
    x = x_ref[...]
    inv = lax.rsqrt(jnp.mean(x * x, axis=-1, keepdims=True) + RMS_EPS)
    inv_ref[...] = jnp.broadcast_to(inv, inv_ref.shape)


def _s5_in_kernel(x_ref, g_ref, o_ref, inv_ref, *, tb, ch):
    nb, _, d = x_ref.shape
    per = LANES // ch
    assert per == SUBLANES and nb % SUBLANES == 0 and S5_CHUNK % SUBLANES == 0
    _inv_rms_to(inv_ref, x_ref)
    piece = lax.broadcasted_iota(jnp.int32, (nb, LANES), 1) // ch
    sub = lax.broadcasted_iota(jnp.int32, (SUBLANES, LANES), 0)

    def tile_body(lt, carry):
        l0 = pl.multiple_of(lt * LANES, LANES)
        gl = g_ref[:, pl.ds(l0, LANES)]
        for c in range(tb):
            by_step = [[None] * (nb // SUBLANES) for _ in range(S5_CHUNK)]
            for bb in range(nb // SUBLANES):
                for tb8 in range(S5_CHUNK // SUBLANES):
                    rows = slice(c * S5_CHUNK + tb8 * SUBLANES, c * S5_CHUNK + (tb8 + 1) * SUBLANES)
                    v = [x_ref[bb * SUBLANES + b, rows, pl.ds(l0, LANES)] * inv_ref[bb * SUBLANES + b, rows, :] * gl
                         for b in range(SUBLANES)]
                    w = _grid_transpose(v, sub, 0, 1)
                    for t in range(SUBLANES):
                        by_step[tb8 * SUBLANES + t][bb] = w[t]
            for h in range(S5_CHUNK // per):
                v = [jnp.concatenate(by_step[h * per + s], axis=0) for s in range(per)]
                w = _grid_transpose(v, piece, 1, ch)
                for gg in range(per):
                    o_ref[lt * per + gg, c * nb:(c + 1) * nb, h * LANES:(h + 1) * LANES] = w[gg].astype(o_ref.dtype)
        return carry

    lax.fori_loop(0, d // LANES, tile_body, 0)


def s5_to_groups(x3, g, *, ch, tb):
    nb, s, d = x3.shape
    rows = tb * S5_CHUNK
    return pl.pallas_call(
        functools.partial(_s5_in_kernel, tb=tb, ch=ch),
        grid=(s // rows,),
        in_specs=[pl.BlockSpec((nb, rows, d), lambda i: (0, i, 0)), pl.BlockSpec((1, d), lambda i: (0, 0))],
        out_specs=pl.BlockSpec((d // ch, tb * nb, S5_CHUNK * ch), lambda i: (0, i, 0)),
        out_shape=jax.ShapeDtypeStruct((d // ch, (s // S5_CHUNK) * nb, S5_CHUNK * ch), BF16),
        scratch_shapes=[pltpu.VMEM((nb, rows, LANES), F32)],
        compiler_params=_params(("parallel",)),
        name="s5_to_groups",
    )(x3, g.reshape(1, d))


def _s5_out_kernel(yg_ref, x_ref, g_ref, d_ref, o_ref, inv_ref, *, tb, ch):
    nb, _, d = x_ref.shape
    per = LANES // ch
    assert per == SUBLANES and nb % SUBLANES == 0 and S5_CHUNK % SUBLANES == 0
    _inv_rms_to(inv_ref, x_ref)
    piece = lax.broadcasted_iota(jnp.int32, (nb, LANES), 1) // ch
    sub = lax.broadcasted_iota(jnp.int32, (SUBLANES, LANES), 0)

    def tile_body(lt, carry):
        l0 = pl.multiple_of(lt * LANES, LANES)
        gl = g_ref[:, pl.ds(l0, LANES)]
        dl = d_ref[:, pl.ds(l0, LANES)]
        for c in range(tb):
            by_step = []
            for h in range(S5_CHUNK // per):
                w = [yg_ref[lt * per + gg, c * nb:(c + 1) * nb, h * LANES:(h + 1) * LANES] for gg in range(per)]
                by_step += _grid_transpose(w, piece, 1, ch)
            for bb in range(nb // SUBLANES):
                acts = [[] for _ in range(SUBLANES)]
                for tb8 in range(S5_CHUNK // SUBLANES):
                    rows = slice(c * S5_CHUNK + tb8 * SUBLANES, c * S5_CHUNK + (tb8 + 1) * SUBLANES)
                    v = [by_step[tb8 * SUBLANES + t][bb * SUBLANES:(bb + 1) * SUBLANES, :] for t in range(SUBLANES)]
                    w = _grid_transpose(v, sub, 0, 1)
                    for b in range(SUBLANES):
                        xn = x_ref[bb * SUBLANES + b, rows, pl.ds(l0, LANES)] * inv_ref[bb * SUBLANES + b, rows, :] * gl
                        u = w[b] + dl * xn
                        acts[b].append(
                            0.5 * u * (1.0 + jnp.tanh(math.sqrt(2.0 / math.pi) * (u + 0.044715 * (u * u * u)))))
                for b in range(SUBLANES):
                    o_ref[bb * SUBLANES + b, c * S5_CHUNK:(c + 1) * S5_CHUNK, pl.ds(l0, LANES)] = (
                        jnp.concatenate(acts[b], axis=0).astype(o_ref.dtype))
        return carry

    lax.fori_loop(0, d // LANES, tile_body, 0)


def s5_from_groups(yg, x3, g, d_skip, *, ch, tb):
    nb, s, d = x3.shape
    rows = tb * S5_CHUNK
    return pl.pallas_call(
        functools.partial(_s5_out_kernel, tb=tb, ch=ch),
        grid=(s // rows,),
        in_specs=[pl.BlockSpec((d // ch, tb * nb, S5_CHUNK * ch), lambda i: (0, i, 0)),
                  pl.BlockSpec((nb, rows, d), lambda i: (0, i, 0)),
                  pl.BlockSpec((1, d), lambda i: (0, 0)),
                  pl.BlockSpec((1, d), lambda i: (0, 0))],
        out_specs=pl.BlockSpec((nb, rows, d), lambda i: (0, i, 0)),
        out_shape=jax.ShapeDtypeStruct((nb, s, d), BF16),
        scratch_shapes=[pltpu.VMEM((nb, rows, LANES), F32)],
        compiler_params=_params(("parallel",)),
        name="s5_from_groups",
    )(yg, x3, g.reshape(1, d), d_skip.reshape(1, d))


def _s5_glu_kernel(a_ref, x_ref, wv_ref, wg_ref, o_ref):
    a = a_ref[...]
    val = jnp.dot(a, wv_ref[...], preferred_element_type=F32)
    gate = jnp.dot(a, wg_ref[...], preferred_element_type=F32)
    o_ref[...] = x_ref[...] + val * jax.nn.sigmoid(gate)


def s5_glu(act, x, w_val, w_gate, *, tm):
    m, d = x.shape
    return pl.pallas_call(
        _s5_glu_kernel,
        grid=(m // tm,),
        in_specs=[pl.BlockSpec((tm, d), lambda i: (i, 0)),
                  pl.BlockSpec((tm, d), lambda i: (i, 0)),
                  _resident(w_val.shape), _resident(w_gate.shape)],
        out_specs=pl.BlockSpec((tm, d), lambda i: (i, 0)),
        out_shape=jax.ShapeDtypeStruct((m, d), F32),
        compiler_params=_params(("parallel",)),
        name="s5_glu",
    )(act, x, w_val, w_gate)


def _mixer_conv_diff(x, norm_g, w_in, conv_dw, conv_db, conv_ln_g, conv_ln_b, lq1, lk1, lq2, lk2, subln_g, w_out,
                     *, batch, seq, lambda_init):
    d_conv = conv_dw.shape[1]
    d_diff = (w_in.shape[1] - 2 * d_conv) // 3
    z = norm_matmul(x, norm_g, w_in.astype(BF16), tm=1024, tn=1024)
    a_out = conformer_conv(z, conv_dw, conv_db, conv_ln_g, conv_ln_b, seq=seq, tr=256)
    b_out = diff_attention(z, lq1, lk1, lq2, lk2, subln_g, batch=batch, seq=seq,
                           q_col=2 * d_conv, k_col=2 * d_conv + d_diff, v_col=2 * d_conv + 2 * d_diff,
                           lambda_init=lambda_init, tq=512)
    return matmul_residual([a_out, b_out], w_out.astype(BF16), x, tm=512)


def _mixer_s5(x, norm_g, lam_re, lam_im, log_dt, b_re, b_im, c_re, c_im, d_skip, w_val, w_gate, *, batch, seq):
    m, d = x.shape
    ch = b_re.shape[-1]
    t_op, e_op, f_op, a_re, a_im = s5_operators(lam_re, lam_im, log_dt, b_re, b_im, c_re, c_im, gb=8)
    x3 = x.reshape(batch, seq, d)
    xg = s5_to_groups(x3, norm_g, ch=ch, tb=4)
    yg = s5_scan(xg, t_op, e_op, f_op, a_re, a_im, nb=batch)
    act = s5_from_groups(yg, x3, norm_g, d_skip, ch=ch, tb=4).reshape(m, d)
    return s5_glu(act, x, w_val.astype(BF16), w_gate.astype(BF16), tm=512)


def _cross_attn_block(x, memf, norm_g, mem_g, wq, wk, wv, wo, *, batch, seq, mem_len):
    d = x.shape[1]
    q = norm_matmul(x, norm_g, wq.astype(BF16), tm=512, tn=d)
    kv = norm_matmul(memf, mem_g, jnp.concatenate([wk, wv], axis=1).astype(BF16), tm=1024, tn=1024)
    o = cross_attention(q, kv, batch=batch, seq=seq, mem_len=mem_len, tq=512)
    return matmul_residual([o], wo.astype(BF16), x, tm=512)


def kernel(x, mem, norm_mix_g, ab_w_in, conv_dw, conv_db, conv_ln_g, conv_ln_b, diff_lq1, diff_lk1, diff_lq2, diff_lk2, diff_subln_g, ab_w_out, s5_lam_re, s5_lam_im, s5_log_dt, s5_b_re, s5_b_im, s5_c_re, s5_c_im, s5_d, s5_w_val, s5_w_gate, norm_xa_g, norm_mem_g, xa_wq, xa_wk, xa_wv, xa_wo, norm_ffn_g, ffn_w_up, ffn_dw, ffn_db, ffn_w_down, final_g):
    batch, seq, d = x.shape
    mem_len = mem.shape[1]
    depth = norm_mix_g.shape[0]
    xf = x.reshape(batch * seq, d)
    memf = mem.reshape(batch * mem_len, d)
    for layer in range(depth):
        i = layer // 2
        if layer % 2 == 0:
            lambda_init = 0.8 - 0.6 * math.exp(-0.3 * layer)
            xf = _mixer_conv_diff(xf, norm_mix_g[layer], ab_w_in[i], conv_dw[i], conv_db[i], conv_ln_g[i],
                                  conv_ln_b[i], diff_lq1[i], diff_lk1[i], diff_lq2[i], diff_lk2[i],
                                  diff_subln_g[i], ab_w_out[i], batch=batch, seq=seq, lambda_init=lambda_init)
        else:
            xf = _mixer_s5(xf, norm_mix_g[layer], s5_lam_re[i], s5_lam_im[i], s5_log_dt[i], s5_b_re[i],
                           s5_b_im[i], s5_c_re[i], s5_c_im[i], s5_d[i], s5_w_val[i], s5_w_gate[i],
                           batch=batch, seq=seq)
        xf = _cross_attn_block(xf, memf, norm_xa_g[layer], norm_mem_g[layer], xa_wq[layer], xa_wk[layer],
                               xa_wv[layer], xa_wo[layer], batch=batch, seq=seq, mem_len=mem_len)
        xf = conv_ffn_block(xf, norm_ffn_g[layer], ffn_w_up[layer].astype(BF16), ffn_dw[layer], ffn_db[layer],
                            ffn_w_down[layer].astype(BF16), final_g, seq=seq, tm=512, tf=512,
                            final_norm=(layer == depth - 1))
    return xf.reshape(batch, seq, d)
```

```python
import functools
import math

import jax
import jax.numpy as jnp
from jax import lax
from jax.experimental import pallas as pl
from jax.experimental.pallas import tpu as pltpu

RMS_EPS = 1e-6
LN_EPS = 1e-5
XA_HEADS = 4
S5_CHUNK = 16
V7X_VMEM_BUDGET = 56 * 1024 * 1024
BF16 = jnp.bfloat16
F32 = jnp.float32


def _params(semantics, vmem=V7X_VMEM_BUDGET):
    return pltpu.CompilerParams(dimension_semantics=semantics, vmem_limit_bytes=vmem)


def _resident(shape):
    return pl.BlockSpec(shape, lambda *_: (0,) * len(shape), pipeline_mode=pl.Buffered(1))


def _rms(x, g, eps):
    return x * lax.rsqrt(jnp.mean(x * x, axis=-1, keepdims=True) + eps) * g


def _norm_matmul_kernel(x_ref, g_ref, w_ref, o_ref, xn_ref):
    @pl.when(pl.program_id(1) == 0)
    def _():
        xn_ref[...] = _rms(x_ref[...], g_ref[...], RMS_EPS).astype(xn_ref.dtype)

    o_ref[...] = jnp.dot(xn_ref[...], w_ref[...], preferred_element_type=F32).astype(o_ref.dtype)


def norm_matmul(x, g, w, *, tm, tn):
    m, k = x.shape
    n = w.shape[1]
    w_spec = _resident((k, n)) if tn == n else pl.BlockSpec((k, tn), lambda i, j: (0, j))
    return pl.pallas_call(
        _norm_matmul_kernel,
        grid=(m // tm, n // tn),
        in_specs=[pl.BlockSpec((tm, k), lambda i, j: (i, 0)),
                  pl.BlockSpec((1, k), lambda i, j: (0, 0)),
                  w_spec],
        out_specs=pl.BlockSpec((tm, tn), lambda i, j: (i, j)),
        out_shape=jax.ShapeDtypeStruct((m, n), BF16),
        scratch_shapes=[pltpu.VMEM((tm, k), BF16)],
        compiler_params=_params(("parallel", "arbitrary")),
        name="norm_matmul",
    )(x, g.reshape(1, k), w)


def _matmul_residual_kernel(*refs, n_lhs):
    a_refs, w_refs = refs[:n_lhs], refs[n_lhs:2 * n_lhs]
    r_ref, o_ref = refs[2 * n_lhs], refs[2 * n_lhs + 1]
    acc = r_ref[...]
    for a_ref, w_ref in zip(a_refs, w_refs):
        acc = acc + jnp.dot(a_ref[...], w_ref[...], preferred_element_type=F32)
    o_ref[...] = acc


def matmul_residual(lhs, w, res, *, tm):
    m, n = res.shape
    kb = w.shape[0] // len(lhs)
    assert all(a.shape[1] == kb for a in lhs)
    in_specs = [pl.BlockSpec((tm, kb), lambda i: (i, 0)) for _ in lhs]
    in_specs += [pl.BlockSpec((kb, n), lambda i, r=r: (r, 0), pipeline_mode=pl.Buffered(1)) for r in range(len(lhs))]
    in_specs += [pl.BlockSpec((tm, n), lambda i: (i, 0))]
    return pl.pallas_call(
        functools.partial(_matmul_residual_kernel, n_lhs=len(lhs)),
        grid=(m // tm,),
        in_specs=in_specs,
        out_specs=pl.BlockSpec((tm, n), lambda i: (i, 0)),
        out_shape=jax.ShapeDtypeStruct((m, n), F32),
        compiler_params=_params(("parallel",)),
        name="matmul_residual",
    )(*lhs, *([w] * len(lhs)), res)


CONV_HALO = 16
SUBLANES = 8


def _conformer_kernel(vp_ref, gp_ref, v_ref, g_ref, vn_ref, gn_ref, dw_ref, db_ref, lg_ref, lb_ref,
                      o_ref, h_ref, *, tr, seq, width, sub):
    i = pl.program_id(0)
    half = width // 2
    has_prev = (i * tr) % seq != 0
    has_next = ((i + 1) * tr) % seq != 0
    rows = tr + 2 * CONV_HALO

    def glu(v, g):
        return v.astype(F32) * jax.nn.sigmoid(g.astype(F32))

    h_ref[0, 0:CONV_HALO, :] = jnp.where(has_prev, glu(vp_ref[...], gp_ref[...]), 0.0)
    h_ref[0, CONV_HALO:CONV_HALO + tr, :] = glu(v_ref[...], g_ref[...])
    h_ref[0, CONV_HALO + tr:, :] = jnp.where(has_next, glu(vn_ref[...], gn_ref[...]), 0.0)
    for b in range(1, SUBLANES):
        h_ref[b, 0:rows - SUBLANES, :] = h_ref[0, b:b + rows - SUBLANES, :]

    def body(r, carry):
        r0 = pl.multiple_of(r * sub, sub)
        acc = jnp.zeros((sub, h_ref.shape[2]), F32) + db_ref[...]
        for k in range(width):
            a, b = divmod(CONV_HALO - half + k, SUBLANES)
            acc = acc + dw_ref[k:k + 1, :] * h_ref[b, pl.ds(r0 + a * SUBLANES, sub), :]
        mu = jnp.mean(acc, axis=-1, keepdims=True)
        cen = acc - mu
        var = jnp.mean(cen * cen, axis=-1, keepdims=True)
        y = cen * lax.rsqrt(var + LN_EPS) * lg_ref[...] + lb_ref[...]
        o_ref[pl.ds(r0, sub), :] = (y * jax.nn.sigmoid(y)).astype(o_ref.dtype)
        return carry

    lax.fori_loop(0, tr // sub, body, 0, unroll=2)


def conformer_conv(z, dw, db, ln_g, ln_b, *, seq, tr):
    m = z.shape[0]
    width, c = dw.shape
    hb = tr // CONV_HALO
    last = m // CONV_HALO - 1
    prev_map = lambda col: (lambda i: (jnp.maximum(i * hb - 1, 0), col))
    next_map = lambda col: (lambda i: (jnp.minimum((i + 1) * hb, last), col))
    main_map = lambda col: (lambda i: (i, col))
    row = lambda a: a.reshape(1, c)
    full = lambda shape: pl.BlockSpec(shape, lambda i: (0, 0))
    return pl.pallas_call(
        functools.partial(_conformer_kernel, tr=tr, seq=seq, width=width, sub=32),
        grid=(m // tr,),
        in_specs=[pl.BlockSpec((CONV_HALO, c), prev_map(0)), pl.BlockSpec((CONV_HALO, c), prev_map(1)),
                  pl.BlockSpec((tr, c), main_map(0)), pl.BlockSpec((tr, c), main_map(1)),
                  pl.BlockSpec((CONV_HALO, c), next_map(0)), pl.BlockSpec((CONV_HALO, c), next_map(1)),
                  full((width, c)), full((1, c)), full((1, c)), full((1, c))],
        out_specs=pl.BlockSpec((tr, c), lambda i: (i, 0)),
        out_shape=jax.ShapeDtypeStruct((m, c), BF16),
        scratch_shapes=[pltpu.VMEM((SUBLANES, tr + 2 * CONV_HALO, c), F32)],
        compiler_params=_params(("parallel",)),
        name="conformer_conv",
    )(z, z, z, z, z, z, dw, row(db), row(ln_g), row(ln_b))


def _diff_attn_kernel(slopes_ref, q_ref, k_ref, v_ref, lq1_ref, lk1_ref, lq2_ref, lk2_ref, sg_ref, o_ref,
                      bias_ref, s0_ref, s1_ref, w0_ref, w1_ref, *, tq, dh, n_tiles, tiles_per_head, nq, pieces, lambda_init):
    g = pl.program_id(0)
    s_len = k_ref.shape[0]
    tile_b = jnp.clip(g - 1, 0, n_tiles - 1)
    s_ref = (s0_ref, s1_ref)
    w_ref = (w0_ref, w1_ref)

    @pl.when(g == 0)
    def _():
        s1_ref[...] = jnp.zeros(s1_ref.shape, s1_ref.dtype)
        w0_ref[...] = jnp.zeros(w0_ref.shape, w0_ref.dtype)

    @pl.when((tile_b % tiles_per_head == 0) & (g <= n_tiles))
    def _():
        row = lax.broadcasted_iota(jnp.int32, bias_ref.shape, 0)
        col = lax.broadcasted_iota(jnp.int32, bias_ref.shape, 1)
        bias_ref[...] = -slopes_ref[tile_b // tiles_per_head] * jnp.abs(row + (s_len - tq) - col).astype(F32)

    def stages(slot_a, slot_b):
        q = q_ref[...]
        lane = lax.broadcasted_iota(jnp.int32, q.shape, 1)
        scale = jnp.asarray(dh ** -0.5, q.dtype)
        qq = jnp.concatenate([jnp.where(mask, q, 0) * scale for mask in (lane < dh, lane >= dh)], axis=0)
        nt = (((1,), (1,)), ((), ()))
        lam = (jnp.exp(jnp.sum(lq1_ref[...] * lk1_ref[...], axis=-1, keepdims=True))
               - jnp.exp(jnp.sum(lq2_ref[...] * lk2_ref[...], axis=-1, keepdims=True)) + lambda_init)
        start = pl.multiple_of((s_len - tq) - (tile_b % nq) * tq, tq)
        kb = s_len // pieces
        rb = tq // pieces
        o = jnp.zeros(o_ref.shape, F32)
        for j in range(pieces):
            keys = slice(j * kb, (j + 1) * kb)
            rows = slice(j * rb, (j + 1) * rb)
            s_both = lax.dot_general(qq, k_ref[keys, :], nt, preferred_element_type=F32)
            s_ref[slot_a][0, :, keys] = s_both[:tq]
            s_ref[slot_a][1, :, keys] = s_both[tq:]
            bias = bias_ref[rows, pl.ds(start, s_len)]

            def softmax_parts(s):
                s = s + bias
                e = jnp.exp(s - jnp.max(s, axis=-1, keepdims=True))
                return e, jnp.sum(e, axis=-1, keepdims=True)

            e1, l1 = softmax_parts(s_ref[slot_b][0, rows, :])
            e2, l2 = softmax_parts(s_ref[slot_b][1, rows, :])
            w_ref[slot_b][rows, :] = (e1 * (1.0 / l1) - e2 * (lam / l2)).astype(w0_ref.dtype)
            o = o + jnp.dot(w_ref[slot_a][:, keys], v_ref[keys, :], preferred_element_type=F32)
        o_ref[...] = (_rms(o, sg_ref[...], LN_EPS) * (1.0 - lambda_init)).astype(o_ref.dtype)

    pl.when(g % 2 == 0)(lambda: stages(0, 1))
    pl.when(g % 2 == 1)(lambda: stages(1, 0))


def diff_attention(z, lq1, lk1, lq2, lk2, subln_g, *, batch, seq, q_col, k_col, v_col, lambda_init, tq):
    dh = lq1.shape[-1]
    dv = subln_g.shape[-1]
    heads = (k_col - q_col) // (2 * dh)
    nq = seq // tq
    qb, kb, vb = q_col // dv, k_col // dv, v_col // dv
    n_tiles = heads * batch * nq
    slopes = jnp.exp2(-8.0 * jnp.arange(1, heads + 1, dtype=F32) / heads)

    def tile(g, lag):
        t = jnp.clip(g - lag, 0, n_tiles - 1)
        return t // (batch * nq), (t // nq) % batch, t % nq

    def q_map(g):
        h, b, i = tile(g, 0)
        return b * nq + i, qb + h

    def k_map(g):
        h, b, _ = tile(g, 0)
        return b, kb + h

    def v_map(g):
        h, b, _ = tile(g, 2)
        return b, vb + h

    def o_map(g):
        h, b, i = tile(g, 2)
        return b * nq + i, h

    vec = lambda a: a.reshape(1, -1)
    small = lambda n: pl.BlockSpec((1, n), lambda g: (0, 0))
    return pl.pallas_call(
        functools.partial(_diff_attn_kernel, tq=tq, dh=dh, n_tiles=n_tiles, tiles_per_head=batch * nq, nq=nq, pieces=8,
                          lambda_init=lambda_init),
        grid=(n_tiles + 2,),
        in_specs=[pl.BlockSpec(memory_space=pltpu.SMEM),
                  pl.BlockSpec((tq, 2 * dh), q_map),
                  pl.BlockSpec((seq, 2 * dh), k_map),
                  pl.BlockSpec((seq, dv), v_map),
                  small(dh), small(dh), small(dh), small(dh), small(dv)],
        out_specs=pl.BlockSpec((tq, dv), o_map),
        out_shape=jax.ShapeDtypeStruct((batch * seq, heads * dv), BF16),
        scratch_shapes=[pltpu.VMEM((tq, 2 * seq - tq), F32),
                        pltpu.VMEM((2, tq, seq), F32), pltpu.VMEM((2, tq, seq), F32),
                        pltpu.VMEM((tq, seq), BF16), pltpu.VMEM((tq, seq), BF16)],
        compiler_params=_params(("arbitrary",)),
        name="diff_attention",
    )(slopes, z, z, z, vec(lq1), vec(lk1), vec(lq2), vec(lk2), vec(subln_g))


def _cross_attn_kernel(q_ref, k_ref, v_ref, o_ref, *, heads):
    dh = q_ref.shape[1] // heads
    nt = (((1,), (1,)), ((), ()))
    for h in range(heads):
        cols = slice(h * dh, (h + 1) * dh)
        s = lax.dot_general(q_ref[:, cols], k_ref[:, cols], nt, preferred_element_type=F32) * (dh ** -0.5)
        e = jnp.exp(s - jnp.max(s, axis=-1, keepdims=True))
        p = e / jnp.sum(e, axis=-1, keepdims=True)
        o_ref[:, cols] = jnp.dot(p.astype(v_ref.dtype), v_ref[:, cols],
                                 preferred_element_type=F32).astype(o_ref.dtype)


def cross_attention(q, kv, *, batch, seq, mem_len, tq):
    d = q.shape[1]
    nq = seq // tq
    return pl.pallas_call(
        functools.partial(_cross_attn_kernel, heads=XA_HEADS),
        grid=(batch, nq),
        in_specs=[pl.BlockSpec((tq, d), lambda b, i: (b * nq + i, 0)),
                  pl.BlockSpec((mem_len, d), lambda b, i: (b, 0)),
                  pl.BlockSpec((mem_len, d), lambda b, i: (b, 1))],
        out_specs=pl.BlockSpec((tq, d), lambda b, i: (b * nq + i, 0)),
        out_shape=jax.ShapeDtypeStruct((batch * seq, d), BF16),
        compiler_params=_params(("parallel", "arbitrary")),
        name="cross_attention",
    )(q, kv, kv)


FFN_HALO = 8


def _ffn_kernel(xp_ref, x_ref, xn_ref, g_ref, wg_ref, wv_ref, dwg_ref, dwv_ref, dbg_ref, dbv_ref, wd_ref,
                fg_ref, o_ref, xs_ref, hg_ref, hv_ref, *, tm, seq, final_norm):
    i = pl.program_id(0)
    f = pl.program_id(1)

    @pl.when(f == 0)
    def _():
        has_prev = (i * tm) % seq != 0
        has_next = ((i + 1) * tm) % seq != 0
        norm = lambda x: _rms(x, g_ref[...], RMS_EPS)
        xs_ref[0:FFN_HALO, :] = jnp.where(has_prev, norm(xp_ref[...]), 0.0).astype(xs_ref.dtype)
        xs_ref[FFN_HALO:FFN_HALO + tm, :] = norm(x_ref[...]).astype(xs_ref.dtype)
        xs_ref[FFN_HALO + tm:, :] = jnp.where(has_next, norm(xn_ref[...]), 0.0).astype(xs_ref.dtype)
        o_ref[...] = x_ref[...]

    xs = xs_ref[...]
    hg_ref[...] = jnp.dot(xs, wg_ref[...], preferred_element_type=F32)
    hv_ref[...] = jnp.dot(xs, wv_ref[...], preferred_element_type=F32)

    def conv(h_ref, dw_ref, db_ref):
        out = db_ref[...] + dw_ref[0:1, :] * h_ref[FFN_HALO - 1:FFN_HALO - 1 + tm, :]
        out = out + dw_ref[1:2, :] * h_ref[FFN_HALO:FFN_HALO + tm, :]
        return out + dw_ref[2:3, :] * h_ref[FFN_HALO + 1:FFN_HALO + 1 + tm, :]

    gate = conv(hg_ref, dwg_ref, dbg_ref)
    act = gate * jax.nn.sigmoid(gate) * conv(hv_ref, dwv_ref, dbv_ref)
    o_ref[...] += jnp.dot(act.astype(wd_ref.dtype), wd_ref[...], preferred_element_type=F32)

    if final_norm:
        @pl.when(f == pl.num_programs(1) - 1)
        def _():
            o_ref[...] = _rms(o_ref[...], fg_ref[...], RMS_EPS)


def conv_ffn_block(x, g, w_up, dw, db, w_down, final_g, *, seq, tm, tf, final_norm):
    m, d = x.shape
    d_ff = w_down.shape[0]
    nf = d_ff // tf
    hb = tm // FFN_HALO
    last = m // FFN_HALO - 1
    db2 = db.reshape(1, 2 * d_ff)
    return pl.pallas_call(
        functools.partial(_ffn_kernel, tm=tm, seq=seq, final_norm=final_norm),
        grid=(m // tm, nf),
        in_specs=[pl.BlockSpec((FFN_HALO, d), lambda i, f: (jnp.maximum(i * hb - 1, 0), 0)),
                  pl.BlockSpec((tm, d), lambda i, f: (i, 0)),
                  pl.BlockSpec((FFN_HALO, d), lambda i, f: (jnp.minimum((i + 1) * hb, last), 0)),
                  pl.BlockSpec((1, d), lambda i, f: (0, 0)),
                  pl.BlockSpec((d, tf), lambda i, f: (0, f)),
                  pl.BlockSpec((d, tf), lambda i, f: (0, nf + f)),
                  pl.BlockSpec((dw.shape[0], tf), lambda i, f: (0, f)),
                  pl.BlockSpec((dw.shape[0], tf), lambda i, f: (0, nf + f)),
                  pl.BlockSpec((1, tf), lambda i, f: (0, f)),
                  pl.BlockSpec((1, tf), lambda i, f: (0, nf + f)),
                  pl.BlockSpec((tf, d), lambda i, f: (f, 0)),
                  pl.BlockSpec((1, d), lambda i, f: (0, 0))],
        out_specs=pl.BlockSpec((tm, d), lambda i, f: (i, 0)),
        out_shape=jax.ShapeDtypeStruct((m, d), F32),
        scratch_shapes=[pltpu.VMEM((tm + 2 * FFN_HALO, d), BF16),
                        pltpu.VMEM((tm + 2 * FFN_HALO, tf), F32),
                        pltpu.VMEM((tm + 2 * FFN_HALO, tf), F32)],
        compiler_params=_params(("parallel", "arbitrary")),
        name="conv_ffn",
    )(x, x, x, g.reshape(1, d), w_up, w_up, dw, dw, db2, db2, w_down, final_g.reshape(1, d))


def _s5_prep_kernel(lr_ref, li_ref, ldt_ref, btr_ref, bti_ref, ctr_ref, cti_ref, cr_ref, ci_ref,
                    vre_ref, vim_ref, kall_ref, wre_ref, wnim_ref, are_ref, aim_ref, *, chunk, ch):
    lr, li = lr_ref[...], li_ref[...]
    dt = jnp.exp(ldt_ref[...])
    mag = jnp.exp(lr * dt)
    lb_re, lb_im = mag * jnp.cos(li * dt), mag * jnp.sin(li * dt)
    den = lr * lr + li * li
    f_re = ((lb_re - 1.0) * lr + lb_im * li) / den
    f_im = (lb_im * lr - (lb_re - 1.0) * li) / den
    br, bi = btr_ref[...], bti_ref[...]
    bb_re = f_re * br - f_im * bi
    bb_im = f_re * bi + f_im * br
    lag = lax.broadcasted_iota(jnp.int32, br.shape, 3) // ch
    p_re, p_im = jnp.ones_like(lr), jnp.zeros_like(lr)
    pw_re, pw_im = jnp.zeros_like(br), jnp.zeros_like(br)
    for t in range(chunk):
        pw_re, pw_im = jnp.where(lag == t, p_re, pw_re), jnp.where(lag == t, p_im, pw_im)
        p_re, p_im = p_re * lb_re - p_im * lb_im, p_re * lb_im + p_im * lb_re
    are_ref[...], aim_ref[...] = p_re, p_im
    pw1_re, pw1_im = pw_re * lb_re - pw_im * lb_im, pw_re * lb_im + pw_im * lb_re
    v_re = pw_re * bb_re - pw_im * bb_im
    v_im = pw_re * bb_im + pw_im * bb_re
    vre_ref[...], vim_ref[...] = v_re.astype(vre_ref.dtype), v_im.astype(vim_ref.dtype)
    ctr, cti = ctr_ref[...], cti_ref[...]
    wre_ref[...] = (pw1_re * ctr - pw1_im * cti).astype(wre_ref.dtype)
    wnim_ref[...] = (-(pw1_re * cti + pw1_im * ctr)).astype(wnim_ref.dtype)
    kdir = []
    for d in range(2):
        kd = (jnp.einsum("gop,gpn->gon", cr_ref[d], v_re[d], precision=lax.Precision.HIGHEST,
                         preferred_element_type=F32)
              - jnp.einsum("gop,gpn->gon", ci_ref[d], v_im[d], precision=lax.Precision.HIGHEST,
                           preferred_element_type=F32))
        kdir.append(kd)
    lag_k = lax.broadcasted_iota(jnp.int32, kdir[0].shape, 2) // ch
    kall_ref[0] = (kdir[0] + jnp.where(lag_k == 0, kdir[1], 0.0)).astype(kall_ref.dtype)
    kall_ref[1] = kdir[1].astype(kall_ref.dtype)


def s5_operators(lam_re, lam_im, log_dt, b_re, b_im, c_re, c_im, *, gb):
    _, g, p = lam_re.shape
    ch = b_re.shape[-1]
    n = S5_CHUNK * ch
    col = lambda a: a.reshape(2, g, p, 1)
    tile = lambda a: jnp.tile(a, (1, 1, 1, S5_CHUNK))
    ct = lambda a: tile(jnp.swapaxes(a, 2, 3))
    spec = lambda *tail: pl.BlockSpec((2, gb) + tail, lambda i: (0, i, 0, 0))
    shp = lambda *tail, dtype=F32: jax.ShapeDtypeStruct((2, g) + tail, dtype)
    big = functools.partial(shp, dtype=BF16)
    vre, vim, kall, wre, wnim, are, aim = pl.pallas_call(
        functools.partial(_s5_prep_kernel, chunk=S5_CHUNK, ch=ch),
        grid=(g // gb,),
        in_specs=[spec(p, 1), spec(p, 1), spec(1, 1), spec(p, n), spec(p, n), spec(p, n), spec(p, n),
                  spec(ch, p), spec(ch, p)],
        out_specs=[spec(p, n), spec(p, n), spec(ch, n), spec(p, n), spec(p, n), spec(p, 1), spec(p, 1)],
        out_shape=[big(p, n), big(p, n), big(ch, n), big(p, n), big(p, n), shp(p, 1), shp(p, 1)],
        compiler_params=_params(("parallel",)),
        name="s5_prep",
    )(col(lam_re), col(lam_im), log_dt.reshape(2, g, 1, 1), tile(b_re), tile(b_im), ct(c_re), ct(c_im),
      c_re, c_im)

    L = S5_CHUNK
    s_idx = jnp.arange(L)[:, None]
    t_idx = jnp.arange(L)[None, :]
    k5 = kall.reshape(2, g, ch, L, ch)
    kf = jnp.take(k5[0], jnp.abs(t_idx - s_idx).reshape(-1), axis=2).reshape(g, ch, L, L, ch)
    kb = jnp.take(k5[1], jnp.abs(t_idx - s_idx).reshape(-1), axis=2).reshape(g, ch, L, L, ch)
    sel = (t_idx >= s_idx)[None, None, :, :, None]
    t_op = jnp.where(sel, kf, kb).transpose(0, 2, 4, 3, 1).reshape(g, n, n)

    v5 = lambda a: a.reshape(2, g, p, L, ch)
    e_f = lambda a: jnp.flip(v5(a)[0], axis=2).transpose(0, 2, 3, 1).reshape(g, n, p)
    e_b = lambda a: v5(a)[1].transpose(0, 2, 3, 1).reshape(g, n, p)
    e_op = jnp.concatenate([e_f(vre), e_b(vre), e_f(vim), e_b(vim)], axis=-1)

    w5 = lambda a: a.reshape(2, g, p, L, ch)
    f_f = lambda a: w5(a)[0].reshape(g, p, n)
    f_b = lambda a: jnp.flip(w5(a)[1], axis=2).reshape(g, p, n)
    f_op = jnp.concatenate([f_f(wre), f_b(wre), f_f(wnim), f_b(wnim)], axis=1)

    lanes = lambda a: jnp.concatenate([a[0, :, :, 0], a[1, :, :, 0]], axis=-1).reshape(g, 1, 2 * p)
    return t_op, e_op, f_op, lanes(are), lanes(aim)


def _s5_group_kernel(x_ref, t_ref, e_ref, f_ref, are_ref, aim_ref, o_ref, s_ref, c_ref, *, nb, n_chunks):
    x = x_ref[0]
    half = e_ref.shape[2] // 2
    s_ref[...] = jnp.dot(x, e_ref[0], preferred_element_type=F32)
    a_re = jnp.broadcast_to(are_ref[0], (nb, half))
    a_im = jnp.broadcast_to(aim_ref[0], (nb, half))
    fwd = lax.broadcasted_iota(jnp.int32, (nb, half), 1) < half // 2

    def step(kk, carry):
        h_re, h_im = carry
        rf = pl.multiple_of(kk * nb, nb)
        rb = pl.multiple_of((n_chunks - 1 - kk) * nb, nb)
        c_ref[pl.ds(rf, nb), 0:half // 2] = h_re[:, 0:half // 2]
        c_ref[pl.ds(rb, nb), half // 2:half] = h_re[:, half // 2:half]
        c_ref[pl.ds(rf, nb), half:half + half // 2] = h_im[:, 0:half // 2]
        c_ref[pl.ds(rb, nb), half + half // 2:] = h_im[:, half // 2:half]
        s_re = jnp.where(fwd, s_ref[pl.ds(rf, nb), 0:half], s_ref[pl.ds(rb, nb), 0:half])
        s_im = jnp.where(fwd, s_ref[pl.ds(rf, nb), half:], s_ref[pl.ds(rb, nb), half:])
        return (a_re * h_re - a_im * h_im + s_re, a_re * h_im + a_im * h_re + s_im)

    zero = jnp.zeros((nb, half), F32)
    lax.fori_loop(0, n_chunks, step, (zero, zero))
    y = jnp.dot(x, t_ref[0], preferred_element_type=F32)
    y = y + jnp.dot(c_ref[...].astype(x.dtype), f_ref[0], preferred_element_type=F32)
    o_ref[0] = y.astype(o_ref.dtype)


def s5_scan(xg, t_op, e_op, f_op, a_re, a_im, *, nb):
    g, rows, n = xg.shape
    grp = lambda *tail: pl.BlockSpec((1,) + tail, lambda i: (i, 0, 0))
    return pl.pallas_call(
        functools.partial(_s5_group_kernel, nb=nb, n_chunks=rows // nb),
        grid=(g,),
        in_specs=[grp(rows, n), grp(n, n), grp(n, n), grp(n, n), grp(1, n // 2), grp(1, n // 2)],
        out_specs=grp(rows, n),
        out_shape=jax.ShapeDtypeStruct((g, rows, n), F32),
        scratch_shapes=[pltpu.VMEM((rows, n), F32), pltpu.VMEM((rows, n), F32)],
        compiler_params=_params(("parallel",)),
        name="s5_scan",
    )(xg, t_op, e_op, f_op, a_re, a_im)


LANES = 128


def _grid_transpose(v, idx, axis, unit):
    n = len(v)
    size = n * unit
    v = list(v)
    d = 1
    while d < n:
        low = (idx & d) == 0
        for j in range(n):
            if j & d == 0:
                a, b = v[j], v[j + d]
                v[j] = jnp.where(low, a, pltpu.roll(b, d * unit, axis))
                v[j + d] = jnp.where(low, pltpu.roll(a, size - d * unit, axis), b)
        d *= 2
    return v


def _inv_rms_to(inv_ref, x_ref):
    x = x_ref[...]
    inv = lax.rsqrt(jnp.mean(x * x, axis=-1, keepdims=True) + RMS_EPS)
    inv_ref[...] = jnp.broadcast_to(inv, inv_ref.shape)


def _s5_in_kernel(x_ref, g_ref, o_ref, inv_ref, *, tb, ch):
    nb, _, d = x_ref.shape
    per = LANES // ch
    assert per == SUBLANES and nb % SUBLANES == 0 and S5_CHUNK % SUBLANES == 0
    _inv_rms_to(inv_ref, x_ref)
    piece = lax.broadcasted_iota(jnp.int32, (nb, LANES), 1) // ch
    sub = lax.broadcasted_iota(jnp.int32, (SUBLANES, LANES), 0)

    def tile_body(lt, carry):
        l0 = pl.multiple_of(lt * LANES, LANES)
        gl = g_ref[:, pl.ds(l0, LANES)]
        for c in range(tb):
            by_step = [[None] * (nb // SUBLANES) for _ in range(S5_CHUNK)]
            for bb in range(nb // SUBLANES):
                for tb8 in range(S5_CHUNK // SUBLANES):
                    rows = slice(c * S5_CHUNK + tb8 * SUBLANES, c * S5_CHUNK + (tb8 + 1) * SUBLANES)
                    v = [x_ref[bb * SUBLANES + b, rows, pl.ds(l0, LANES)] * inv_ref[bb * SUBLANES + b, rows, :] * gl
                         for b in range(SUBLANES)]
                    w = _grid_transpose(v, sub, 0, 1)
                    for t in range(SUBLANES):
                        by_step[tb8 * SUBLANES + t][bb] = w[t]
            for h in range(S5_CHUNK // per):
                v = [jnp.concatenate(by_step[h * per + s], axis=0) for s in range(per)]
                w = _grid_transpose(v, piece, 1, ch)
                for gg in range(per):
                    o_ref[lt * per + gg, c * nb:(c + 1) * nb, h * LANES:(h + 1) * LANES] = w[gg].astype(o_ref.dtype)
        return carry

    lax.fori_loop(0, d // LANES, tile_body, 0)


def s5_to_groups(x3, g, *, ch, tb):
    nb, s, d = x3.shape
    rows = tb * S5_CHUNK
    return pl.pallas_call(
        functools.partial(_s5_in_kernel, tb=tb, ch=ch),
        grid=(s // rows,),
        in_specs=[pl.BlockSpec((nb, rows, d), lambda i: (0, i, 0)), pl.BlockSpec((1, d), lambda i: (0, 0))],
        out_specs=pl.BlockSpec((d // ch, tb * nb, S5_CHUNK * ch), lambda i: (0, i, 0)),
        out_shape=jax.ShapeDtypeStruct((d // ch, (s // S5_CHUNK) * nb, S5_CHUNK * ch), BF16),
        scratch_shapes=[pltpu.VMEM((nb, rows, LANES), F32)],
        compiler_params=_params(("parallel",)),
        name="s5_to_groups",
    )(x3, g.reshape(1, d))


def _s5_out_kernel(yg_ref, x_ref, g_ref, d_ref, o_ref, inv_ref, *, tb, ch):
    nb, _, d = x_ref.shape
    per = LANES // ch
    assert per == SUBLANES and nb % SUBLANES == 0 and S5_CHUNK % SUBLANES == 0
    _inv_rms_to(inv_ref, x_ref)
    piece = lax.broadcasted_iota(jnp.int32, (nb, LANES), 1) // ch
    sub = lax.broadcasted_iota(jnp.int32, (SUBLANES, LANES), 0)

    def tile_body(lt, carry):
        l0 = pl.multiple_of(lt * LANES, LANES)
        gl = g_ref[:, pl.ds(l0, LANES)]
        dl = d_ref[:, pl.ds(l0, LANES)]
        for c in range(tb):
            by_step = []
            for h in range(S5_CHUNK // per):
                w = [yg_ref[lt * per + gg, c * nb:(c + 1) * nb, h * LANES:(h + 1) * LANES] for gg in range(per)]
                by_step += _grid_transpose(w, piece, 1, ch)
            for bb in range(nb // SUBLANES):
                acts = [[] for _ in range(SUBLANES)]
                for tb8 in range(S5_CHUNK // SUBLANES):
                    rows = slice(c * S5_CHUNK + tb8 * SUBLANES, c * S5_CHUNK + (tb8 + 1) * SUBLANES)
                    v = [by_step[tb8 * SUBLANES + t][bb * SUBLANES:(bb + 1) * SUBLANES, :] for t in range(SUBLANES)]
                    w = _grid_transpose(v, sub, 0, 1)
                    for b in range(SUBLANES):
                        xn = x_ref[bb * SUBLANES + b, rows, pl.ds(l0, LANES)] * inv_ref[bb * SUBLANES + b, rows, :] * gl
                        u = w[b] + dl * xn
                        acts[b].append(
                            0.5 * u * (1.0 + jnp.tanh(math.sqrt(2.0 / math.pi) * (u + 0.044715 * (u * u * u)))))
                for b in range(SUBLANES):
                    o_ref[bb * SUBLANES + b, c * S5_CHUNK:(c + 1) * S5_CHUNK, pl.ds(l0, LANES)] = (
                        jnp.concatenate(acts[b], axis=0).astype(o_ref.dtype))
        return carry

    lax.fori_loop(0, d // LANES, tile_body, 0)


def s5_from_groups(yg, x3, g, d_skip, *, ch, tb):
    nb, s, d = x3.shape
    rows = tb * S5_CHUNK
    return pl.pallas_call(
        functools.partial(_s5_out_kernel, tb=tb, ch=ch),
        grid=(s // rows,),
        in_specs=[pl.BlockSpec((d // ch, tb * nb, S5_CHUNK * ch), lambda i: (0, i, 0)),
                  pl.BlockSpec((nb, rows, d), lambda i: (0, i, 0)),
                  pl.BlockSpec((1, d), lambda i: (0, 0)),
                  pl.BlockSpec((1, d), lambda i: (0, 0))],
        out_specs=pl.BlockSpec((nb, rows, d), lambda i: (0, i, 0)),
        out_shape=jax.ShapeDtypeStruct((nb, s, d), BF16),
        scratch_shapes=[pltpu.VMEM((nb, rows, LANES), F32)],
        compiler_params=_params(("parallel",)),
        name="s5_from_groups",
    )(yg, x3, g.reshape(1, d), d_skip.reshape(1, d))


def _s5_glu_kernel(a_ref, x_ref, wv_ref, wg_ref, o_ref):
    a = a_ref[...]
    val = jnp.dot(a, wv_ref[...], preferred_element_type=F32)
    gate = jnp.dot(a, wg_ref[...], preferred_element_type=F32)
    o_ref[...] = x_ref[...] + val * jax.nn.sigmoid(gate)


def s5_glu(act, x, w_val, w_gate, *, tm):
    m, d = x.shape
    return pl.pallas_call(
        _s5_glu_kernel,
        grid=(m // tm,),
        in_specs=[pl.BlockSpec((tm, d), lambda i: (i, 0)),
                  pl.BlockSpec((tm, d), lambda i: (i, 0)),
                  _resident(w_val.shape), _resident(w_gate.shape)],
        out_specs=pl.BlockSpec((tm, d), lambda i: (i, 0)),
        out_shape=jax.ShapeDtypeStruct((m, d), F32),
        compiler_params=_params(("parallel",)),
        name="s5_glu",
    )(act, x, w_val, w_gate)


def _mixer_conv_diff(x, norm_g, w_in, conv_dw, conv_db, conv_ln_g, conv_ln_b, lq1, lk1, lq2, lk2, subln_g, w_out,
                     *, batch, seq, lambda_init):
    d_conv = conv_dw.shape[1]
    d_diff = (w_in.shape[1] - 2 * d_conv) // 3
    z = norm_matmul(x, norm_g, w_in.astype(BF16), tm=1024, tn=1280)
    a_out = conformer_conv(z, conv_dw, conv_db, conv_ln_g, conv_ln_b, seq=seq, tr=256)
    b_out = diff_attention(z, lq1, lk1, lq2, lk2, subln_g, batch=batch, seq=seq,
                           q_col=2 * d_conv, k_col=2 * d_conv + d_diff, v_col=2 * d_conv + 2 * d_diff,
                           lambda_init=lambda_init, tq=512)
    return matmul_residual([a_out, b_out], w_out.astype(BF16), x, tm=512)


def _mixer_s5(x, norm_g, lam_re, lam_im, log_dt, b_re, b_im, c_re, c_im, d_skip, w_val, w_gate, *, batch, seq):
    m, d = x.shape
    ch = b_re.shape[-1]
    t_op, e_op, f_op, a_re, a_im = s5_operators(lam_re, lam_im, log_dt, b_re, b_im, c_re, c_im, gb=8)
    x3 = x.reshape(batch, seq, d)
    xg = s5_to_groups(x3, norm_g, ch=ch, tb=4)
    yg = s5_scan(xg, t_op, e_op, f_op, a_re, a_im, nb=batch)
    act = s5_from_groups(yg, x3, norm_g, d_skip, ch=ch, tb=4).reshape(m, d)
    return s5_glu(act, x, w_val.astype(BF16), w_gate.astype(BF16), tm=512)


def _cross_attn_block(x, memf, norm_g, mem_g, wq, wk, wv, wo, *, batch, seq, mem_len):
    d = x.shape[1]
    q = norm_matmul(x, norm_g, wq.astype(BF16), tm=512, tn=d)
    kv = norm_matmul(memf, mem_g, jnp.concatenate([wk, wv], axis=1).astype(BF16), tm=1024, tn=1024)
    o = cross_attention(q, kv, batch=batch, seq=seq, mem_len=mem_len, tq=512)
    return matmul_residual([o], wo.astype(BF16), x, tm=512)


def kernel(x, mem, norm_mix_g, ab_w_in, conv_dw, conv_db, conv_ln_g, conv_ln_b, diff_lq1, diff_lk1, diff_lq2, diff_lk2, diff_subln_g, ab_w_out, s5_lam_re, s5_lam_im, s5_log_dt, s5_b_re, s5_b_im, s5_c_re, s5_c_im, s5_d, s5_w_val, s5_w_gate, norm_xa_g, norm_mem_g, xa_wq, xa_wk, xa_wv, xa_wo, norm_ffn_g, ffn_w_up, ffn_dw, ffn_db, ffn_w_down, final_g):
    batch, seq, d = x.shape
    mem_len = mem.shape[1]
    depth = norm_mix_g.shape[0]
    xf = x.reshape(batch * seq, d)
    memf = mem.reshape(batch * mem_len, d)
    for layer in range(depth):
        i = layer // 2
        if layer % 2 == 0:
            lambda_init = 0.8 - 0.6 * math.exp(-0.3 * layer)
            xf = _mixer_conv_diff(xf, norm_mix_g[layer], ab_w_in[i], conv_dw[i], conv_db[i], conv_ln_g[i],
                                  conv_ln_b[i], diff_lq1[i], diff_lk1[i], diff_lq2[i], diff_lk2[i],
                                  diff_subln_g[i], ab_w_out[i], batch=batch, seq=seq, lambda_init=lambda_init)
        else:
            xf = _mixer_s5(xf, norm_mix_g[layer], s5_lam_re[i], s5_lam_im[i], s5_log_dt[i], s5_b_re[i],
                           s5_b_im[i], s5_c_re[i], s5_c_im[i], s5_d[i], s5_w_val[i], s5_w_gate[i],
                           batch=batch, seq=seq)
        xf = _cross_attn_block(xf, memf, norm_xa_g[layer], norm_mem_g[layer], xa_wq[layer], xa_wk[layer],
                               xa_wv[layer], xa_wo[layer], batch=batch, seq=seq, mem_len=mem_len)
        xf = conv_ffn_block(xf, norm_ffn_g[layer], ffn_w_up[layer].astype(BF16), ffn_dw[layer], ffn_db[layer],
                            ffn_w_down[layer].astype(BF16), final_g, seq=seq, tm=512, tf=512,
                            final_norm=(layer == depth - 1))
    return xf.reshape(batch, seq, d)
```

```python
import functools
import math

import jax
import jax.numpy as jnp
from jax import lax
from jax.experimental import pallas as pl
from jax.experimental.pallas import tpu as pltpu

RMS_EPS = 1e-6
LN_EPS = 1e-5
XA_HEADS = 4
S5_CHUNK = 16
V7X_VMEM_BUDGET = 56 * 1024 * 1024
BF16 = jnp.bfloat16
F32 = jnp.float32


def _params(semantics, vmem=V7X_VMEM_BUDGET):
    return pltpu.CompilerParams(dimension_semantics=semantics, vmem_limit_bytes=vmem)


def _resident(shape):
    return pl.BlockSpec(shape, lambda *_: (0,) * len(shape), pipeline_mode=pl.Buffered(1))


def _rms(x, g, eps):
    return x * lax.rsqrt(jnp.mean(x * x, axis=-1, keepdims=True) + eps) * g


CAST_BLOCK_ELEMS = 1 << 20


def _cast_kernel(w_ref, o_ref):
    o_ref[...] = w_ref[...].astype(o_ref.dtype)


def layer_bf16(w, idx):
    _, k, n = w.shape
    tk = 16
    while tk * 2 * n <= CAST_BLOCK_ELEMS and k % (tk * 2) == 0:
        tk *= 2
    return pl.pallas_call(
        _cast_kernel,
        grid=(k // tk,),
        in_specs=[pl.BlockSpec((None, tk, n), lambda i: (idx, i, 0))],
        out_specs=pl.BlockSpec((tk, n), lambda i: (i, 0)),
        out_shape=jax.ShapeDtypeStruct((k, n), BF16),
        compiler_params=_params(("parallel",)),
        name="layer_bf16",
    )(w)


def _norm_matmul_kernel(x_ref, g_ref, w_ref, o_ref, xn_ref):
    @pl.when(pl.program_id(1) == 0)
    def _():
        xn_ref[...] = _rms(x_ref[...], g_ref[...], RMS_EPS).astype(xn_ref.dtype)

    o_ref[...] = jnp.dot(xn_ref[...], w_ref[...], preferred_element_type=F32).astype(o_ref.dtype)


def norm_matmul(x, g, w, *, tm, tn):
    m, k = x.shape
    n = w.shape[1]
    w_spec = _resident((k, n)) if tn == n else pl.BlockSpec((k, tn), lambda i, j: (0, j))
    return pl.pallas_call(
        _norm_matmul_kernel,
        grid=(m // tm, n // tn),
        in_specs=[pl.BlockSpec((tm, k), lambda i, j: (i, 0)),
                  pl.BlockSpec((1, k), lambda i, j: (0, 0)),
                  w_spec],
        out_specs=pl.BlockSpec((tm, tn), lambda i, j: (i, j)),
        out_shape=jax.ShapeDtypeStruct((m, n), BF16),
        scratch_shapes=[pltpu.VMEM((tm, k), BF16)],
        compiler_params=_params(("parallel", "arbitrary")),
        name="norm_matmul",
    )(x, g.reshape(1, k), w)


def _matmul_residual_kernel(*refs, n_lhs):
    a_refs, w_refs = refs[:n_lhs], refs[n_lhs:2 * n_lhs]
    r_ref, o_ref = refs[2 * n_lhs], refs[2 * n_lhs + 1]
    acc = r_ref[...]
    for a_ref, w_ref in zip(a_refs, w_refs):
        acc = acc + jnp.dot(a_ref[...], w_ref[...], preferred_element_type=F32)
    o_ref[...] = acc


def matmul_residual(lhs, w, res, *, tm):
    m, n = res.shape
    kb = w.shape[0] // len(lhs)
    assert all(a.shape[1] == kb for a in lhs)
    in_specs = [pl.BlockSpec((tm, kb), lambda i: (i, 0)) for _ in lhs]
    in_specs += [pl.BlockSpec((kb, n), lambda i, r=r: (r, 0), pipeline_mode=pl.Buffered(1)) for r in range(len(lhs))]
    in_specs += [pl.BlockSpec((tm, n), lambda i: (i, 0))]
    return pl.pallas_call(
        functools.partial(_matmul_residual_kernel, n_lhs=len(lhs)),
        grid=(m // tm,),
        in_specs=in_specs,
        out_specs=pl.BlockSpec((tm, n), lambda i: (i, 0)),
        out_shape=jax.ShapeDtypeStruct((m, n), F32),
        compiler_params=_params(("parallel",)),
        name="matmul_residual",
    )(*lhs, *([w] * len(lhs)), res)


CONV_HALO = 16
SUBLANES = 8


def _conformer_kernel(vp_ref, gp_ref, v_ref, g_ref, vn_ref, gn_ref, dw_ref, db_ref, lg_ref, lb_ref,
                      o_ref, h_ref, *, tr, seq, width, sub):
    i = pl.program_id(0)
    half = width // 2
    has_prev = (i * tr) % seq != 0
    has_next = ((i + 1) * tr) % seq != 0
    rows = tr + 2 * CONV_HALO

    def glu(v, g):
        return v.astype(F32) * jax.nn.sigmoid(g.astype(F32))

    h_ref[0, 0:CONV_HALO, :] = jnp.where(has_prev, glu(vp_ref[...], gp_ref[...]), 0.0)
    h_ref[0, CONV_HALO:CONV_HALO + tr, :] = glu(v_ref[...], g_ref[...])
    h_ref[0, CONV_HALO + tr:, :] = jnp.where(has_next, glu(vn_ref[...], gn_ref[...]), 0.0)
    for b in range(1, SUBLANES):
        h_ref[b, 0:rows - SUBLANES, :] = h_ref[0, b:b + rows - SUBLANES, :]

    def body(r, carry):
        r0 = pl.multiple_of(r * sub, sub)
        acc = jnp.zeros((sub, h_ref.shape[2]), F32) + db_ref[...]
        for k in range(width):
            a, b = divmod(CONV_HALO - half + k, SUBLANES)
            acc = acc + dw_ref[k:k + 1, :] * h_ref[b, pl.ds(r0 + a * SUBLANES, sub), :]
        mu = jnp.mean(acc, axis=-1, keepdims=True)
        cen = acc - mu
        var = jnp.mean(cen * cen, axis=-1, keepdims=True)
        y = cen * lax.rsqrt(var + LN_EPS) * lg_ref[...] + lb_ref[...]
        o_ref[pl.ds(r0, sub), :] = (y * jax.nn.sigmoid(y)).astype(o_ref.dtype)
        return carry

    lax.fori_loop(0, tr // sub, body, 0, unroll=2)


def conformer_conv(z, dw, db, ln_g, ln_b, *, seq, tr):
    m = z.shape[0]
    width, c = dw.shape
    hb = tr // CONV_HALO
    last = m // CONV_HALO - 1
    prev_map = lambda col: (lambda i: (jnp.maximum(i * hb - 1, 0), col))
    next_map = lambda col: (lambda i: (jnp.minimum((i + 1) * hb, last), col))
    main_map = lambda col: (lambda i: (i, col))
    row = lambda a: a.reshape(1, c)
    full = lambda shape: pl.BlockSpec(shape, lambda i: (0, 0))
    return pl.pallas_call(
        functools.partial(_conformer_kernel, tr=tr, seq=seq, width=width, sub=32),
        grid=(m // tr,),
        in_specs=[pl.BlockSpec((CONV_HALO, c), prev_map(0)), pl.BlockSpec((CONV_HALO, c), prev_map(1)),
                  pl.BlockSpec((tr, c), main_map(0)), pl.BlockSpec((tr, c), main_map(1)),
                  pl.BlockSpec((CONV_HALO, c), next_map(0)), pl.BlockSpec((CONV_HALO, c), next_map(1)),
                  full((width, c)), full((1, c)), full((1, c)), full((1, c))],
        out_specs=pl.BlockSpec((tr, c), lambda i: (i, 0)),
        out_shape=jax.ShapeDtypeStruct((m, c), BF16),
        scratch_shapes=[pltpu.VMEM((SUBLANES, tr + 2 * CONV_HALO, c), F32)],
        compiler_params=_params(("parallel",)),
        name="conformer_conv",
    )(z, z, z, z, z, z, dw, row(db), row(ln_g), row(ln_b))


def _diff_attn_kernel(slopes_ref, q_ref, k_ref, v_ref, lq1_ref, lk1_ref, lq2_ref, lk2_ref, sg_ref, o_ref,
                      bias_ref, s0_ref, s1_ref, w0_ref, w1_ref, *, tq, dh, n_tiles, tiles_per_head, nq, pieces, lambda_init):
    g = pl.program_id(0)
    s_len = k_ref.shape[0]
    tile_b = jnp.clip(g - 1, 0, n_tiles - 1)
    s_ref = (s0_ref, s1_ref)
    w_ref = (w0_ref, w1_ref)

    @pl.when(g == 0)
    def _():
        s1_ref[...] = jnp.zeros(s1_ref.shape, s1_ref.dtype)
        w0_ref[...] = jnp.zeros(w0_ref.shape, w0_ref.dtype)

    @pl.when((tile_b % tiles_per_head == 0) & (g <= n_tiles))
    def _():
        row = lax.broadcasted_iota(jnp.int32, bias_ref.shape, 0)
        col = lax.broadcasted_iota(jnp.int32, bias_ref.shape, 1)
        bias_ref[...] = -slopes_ref[tile_b // tiles_per_head] * jnp.abs(row + (s_len - tq) - col).astype(F32)

    def stages(slot_a, slot_b):
        q = q_ref[...]
        lane = lax.broadcasted_iota(jnp.int32, q.shape, 1)
        scale = jnp.asarray(dh ** -0.5, q.dtype)
        qq = jnp.concatenate([jnp.where(mask, q, 0) * scale for mask in (lane < dh, lane >= dh)], axis=0)
        nt = (((1,), (1,)), ((), ()))
        lam = (jnp.exp(jnp.sum(lq1_ref[...] * lk1_ref[...], axis=-1, keepdims=True))
               - jnp.exp(jnp.sum(lq2_ref[...] * lk2_ref[...], axis=-1, keepdims=True)) + lambda_init)
        start = pl.multiple_of((s_len - tq) - (tile_b % nq) * tq, tq)
        kb = s_len // pieces
        rb = tq // pieces
        o = jnp.zeros(o_ref.shape, F32)
        for j in range(pieces):
            keys = slice(j * kb, (j + 1) * kb)
            rows = slice(j * rb, (j + 1) * rb)
            s_both = lax.dot_general(qq, k_ref[keys, :], nt, preferred_element_type=F32)
            s_ref[slot_a][0, :, keys] = s_both[:tq]
            s_ref[slot_a][1, :, keys] = s_both[tq:]
            bias = bias_ref[rows, pl.ds(start, s_len)]

            def softmax_parts(s):
                s = s + bias
                e = jnp.exp(s - jnp.max(s, axis=-1, keepdims=True))
                return e, jnp.sum(e, axis=-1, keepdims=True)

            e1, l1 = softmax_parts(s_ref[slot_b][0, rows, :])
            e2, l2 = softmax_parts(s_ref[slot_b][1, rows, :])
            w_ref[slot_b][rows, :] = (e1 * (1.0 / l1) - e2 * (lam / l2)).astype(w0_ref.dtype)
            o = o + jnp.dot(w_ref[slot_a][:, keys], v_ref[keys, :], preferred_element_type=F32)
        o_ref[...] = (_rms(o, sg_ref[...], LN_EPS) * (1.0 - lambda_init)).astype(o_ref.dtype)

    pl.when(g % 2 == 0)(lambda: stages(0, 1))
    pl.when(g % 2 == 1)(lambda: stages(1, 0))


def diff_attention(z, lq1, lk1, lq2, lk2, subln_g, *, batch, seq, q_col, k_col, v_col, lambda_init, tq):
    dh = lq1.shape[-1]
    dv = subln_g.shape[-1]
    heads = (k_col - q_col) // (2 * dh)
    nq = seq // tq
    qb, kb, vb = q_col // dv, k_col // dv, v_col // dv
    n_tiles = heads * batch * nq
    slopes = jnp.exp2(-8.0 * jnp.arange(1, heads + 1, dtype=F32) / heads)

    def tile(g, lag):
        t = jnp.clip(g - lag, 0, n_tiles - 1)
        return t // (batch * nq), (t // nq) % batch, t % nq

    def q_map(g):
        h, b, i = tile(g, 0)
        return b * nq + i, qb + h

    def k_map(g):
        h, b, _ = tile(g, 0)
        return b, kb + h

    def v_map(g):
        h, b, _ = tile(g, 2)
        return b, vb + h

    def o_map(g):
        h, b, i = tile(g, 2)
        return b * nq + i, h

    vec = lambda a: a.reshape(1, -1)
    small = lambda n: pl.BlockSpec((1, n), lambda g: (0, 0))
    return pl.pallas_call(
        functools.partial(_diff_attn_kernel, tq=tq, dh=dh, n_tiles=n_tiles, tiles_per_head=batch * nq, nq=nq, pieces=8,
                          lambda_init=lambda_init),
        grid=(n_tiles + 2,),
        in_specs=[pl.BlockSpec(memory_space=pltpu.SMEM),
                  pl.BlockSpec((tq, 2 * dh), q_map),
                  pl.BlockSpec((seq, 2 * dh), k_map),
                  pl.BlockSpec((seq, dv), v_map),
                  small(dh), small(dh), small(dh), small(dh), small(dv)],
        out_specs=pl.BlockSpec((tq, dv), o_map),
        out_shape=jax.ShapeDtypeStruct((batch * seq, heads * dv), BF16),
        scratch_shapes=[pltpu.VMEM((tq, 2 * seq - tq), F32),
                        pltpu.VMEM((2, tq, seq), F32), pltpu.VMEM((2, tq, seq), F32),
                        pltpu.VMEM((tq, seq), BF16), pltpu.VMEM((tq, seq), BF16)],
        compiler_params=_params(("arbitrary",)),
        name="diff_attention",
    )(slopes, z, z, z, vec(lq1), vec(lk1), vec(lq2), vec(lk2), vec(subln_g))


def _cross_attn_kernel(q_ref, k_ref, v_ref, o_ref, *, heads):
    dh = q_ref.shape[1] // heads
    nt = (((1,), (1,)), ((), ()))
    cols = [slice(h * dh, (h + 1) * dh) for h in range(heads)]
    s = [lax.dot_general(q_ref[:, c], k_ref[:, c], nt, preferred_element_type=F32) * (dh ** -0.5) for c in cols]
    m = [jnp.max(x, axis=-1, keepdims=True) for x in s]
    e = [jnp.exp(x - mx) for x, mx in zip(s, m)]
    l = [jnp.sum(x, axis=-1, keepdims=True) for x in e]
    p = [(x / lx).astype(v_ref.dtype) for x, lx in zip(e, l)]
    for c, ph in zip(cols, p):
        o_ref[:, c] = jnp.dot(ph, v_ref[:, c], preferred_element_type=F32).astype(o_ref.dtype)


def cross_attention(q, kv, *, batch, seq, mem_len, tq):
    d = q.shape[1]
    nq = seq // tq
    return pl.pallas_call(
        functools.partial(_cross_attn_kernel, heads=XA_HEADS),
        grid=(batch, nq),
        in_specs=[pl.BlockSpec((tq, d), lambda b, i: (b * nq + i, 0)),
                  pl.BlockSpec((mem_len, d), lambda b, i: (b, 0)),
                  pl.BlockSpec((mem_len, d), lambda b, i: (b, 1))],
        out_specs=pl.BlockSpec((tq, d), lambda b, i: (b * nq + i, 0)),
        out_shape=jax.ShapeDtypeStruct((batch * seq, d), BF16),
        compiler_params=_params(("parallel", "arbitrary")),
        name="cross_attention",
    )(q, kv, kv)


FFN_HALO = 8


def _ffn_kernel(xp_ref, x_ref, xn_ref, g_ref, wg_ref, wv_ref, dwg_ref, dwv_ref, dbg_ref, dbv_ref, wd_ref,
                fg_ref, o_ref, xs_ref, hg_ref, hv_ref, *, tm, seq, final_norm):
    i = pl.program_id(0)
    f = pl.program_id(1)

    @pl.when(f == 0)
    def _():
        has_prev = (i * tm) % seq != 0
        has_next = ((i + 1) * tm) % seq != 0
        norm = lambda x: _rms(x, g_ref[...], RMS_EPS)
        xs_ref[0:FFN_HALO, :] = jnp.where(has_prev, norm(xp_ref[...]), 0.0).astype(xs_ref.dtype)
        xs_ref[FFN_HALO:FFN_HALO + tm, :] = norm(x_ref[...]).astype(xs_ref.dtype)
        xs_ref[FFN_HALO + tm:, :] = jnp.where(has_next, norm(xn_ref[...]), 0.0).astype(xs_ref.dtype)
        o_ref[...] = x_ref[...]

    xs = xs_ref[...]
    hg_ref[...] = jnp.dot(xs, wg_ref[...], preferred_element_type=F32)
    hv_ref[...] = jnp.dot(xs, wv_ref[...], preferred_element_type=F32)

    def conv(h_ref, dw_ref, db_ref):
        out = db_ref[...] + dw_ref[0:1, :] * h_ref[FFN_HALO - 1:FFN_HALO - 1 + tm, :]
        out = out + dw_ref[1:2, :] * h_ref[FFN_HALO:FFN_HALO + tm, :]
        return out + dw_ref[2:3, :] * h_ref[FFN_HALO + 1:FFN_HALO + 1 + tm, :]

    gate = conv(hg_ref, dwg_ref, dbg_ref)
    act = gate * jax.nn.sigmoid(gate) * conv(hv_ref, dwv_ref, dbv_ref)
    o_ref[...] += jnp.dot(act.astype(wd_ref.dtype), wd_ref[...], preferred_element_type=F32)

    if final_norm:
        @pl.when(f == pl.num_programs(1) - 1)
        def _():
            o_ref[...] = _rms(o_ref[...], fg_ref[...], RMS_EPS)


def conv_ffn_block(x, g, w_up, dw, db, w_down, final_g, *, seq, tm, tf, final_norm):
    m, d = x.shape
    d_ff = w_down.shape[0]
    nf = d_ff // tf
    hb = tm // FFN_HALO
    last = m // FFN_HALO - 1
    db2 = db.reshape(1, 2 * d_ff)
    return pl.pallas_call(
        functools.partial(_ffn_kernel, tm=tm, seq=seq, final_norm=final_norm),
        grid=(m // tm, nf),
        in_specs=[pl.BlockSpec((FFN_HALO, d), lambda i, f: (jnp.maximum(i * hb - 1, 0), 0)),
                  pl.BlockSpec((tm, d), lambda i, f: (i, 0)),
                  pl.BlockSpec((FFN_HALO, d), lambda i, f: (jnp.minimum((i + 1) * hb, last), 0)),
                  pl.BlockSpec((1, d), lambda i, f: (0, 0)),
                  pl.BlockSpec((d, tf), lambda i, f: (0, f)),
                  pl.BlockSpec((d, tf), lambda i, f: (0, nf + f)),
                  pl.BlockSpec((dw.shape[0], tf), lambda i, f: (0, f)),
                  pl.BlockSpec((dw.shape[0], tf), lambda i, f: (0, nf + f)),
                  pl.BlockSpec((1, tf), lambda i, f: (0, f)),
                  pl.BlockSpec((1, tf), lambda i, f: (0, nf + f)),
                  pl.BlockSpec((tf, d), lambda i, f: (f, 0)),
                  pl.BlockSpec((1, d), lambda i, f: (0, 0))],
        out_specs=pl.BlockSpec((tm, d), lambda i, f: (i, 0)),
        out_shape=jax.ShapeDtypeStruct((m, d), F32),
        scratch_shapes=[pltpu.VMEM((tm + 2 * FFN_HALO, d), BF16),
                        pltpu.VMEM((tm + 2 * FFN_HALO, tf), F32),
                        pltpu.VMEM((tm + 2 * FFN_HALO, tf), F32)],
        compiler_params=_params(("parallel", "arbitrary")),
        name="conv_ffn",
    )(x, x, x, g.reshape(1, d), w_up, w_up, dw, dw, db2, db2, w_down, final_g.reshape(1, d))


def _s5_prep_kernel(lr_ref, li_ref, ldt_ref, btr_ref, bti_ref, ctr_ref, cti_ref, cr_ref, ci_ref,
                    vre_ref, vim_ref, kall_ref, wre_ref, wnim_ref, are_ref, aim_ref, *, chunk, ch):
    lr, li = lr_ref[...], li_ref[...]
    dt = jnp.exp(ldt_ref[...])
    mag = jnp.exp(lr * dt)
    lb_re, lb_im = mag * jnp.cos(li * dt), mag * jnp.sin(li * dt)
    den = lr * lr + li * li
    f_re = ((lb_re - 1.0) * lr + lb_im * li) / den
    f_im = (lb_im * lr - (lb_re - 1.0) * li) / den
    br, bi = btr_ref[...], bti_ref[...]
    bb_re = f_re * br - f_im * bi
    bb_im = f_re * bi + f_im * br
    lag = lax.broadcasted_iota(jnp.int32, br.shape, 3) // ch
    p_re, p_im = jnp.ones_like(lr), jnp.zeros_like(lr)
    pw_re, pw_im = jnp.zeros_like(br), jnp.zeros_like(br)
    for t in range(chunk):
        pw_re, pw_im = jnp.where(lag == t, p_re, pw_re), jnp.where(lag == t, p_im, pw_im)
        p_re, p_im = p_re * lb_re - p_im * lb_im, p_re * lb_im + p_im * lb_re
    are_ref[...], aim_ref[...] = p_re, p_im
    pw1_re, pw1_im = pw_re * lb_re - pw_im * lb_im, pw_re * lb_im + pw_im * lb_re
    v_re = pw_re * bb_re - pw_im * bb_im
    v_im = pw_re * bb_im + pw_im * bb_re
    vre_ref[...], vim_ref[...] = v_re.astype(vre_ref.dtype), v_im.astype(vim_ref.dtype)
    ctr, cti = ctr_ref[...], cti_ref[...]
    wre_ref[...] = (pw1_re * ctr - pw1_im * cti).astype(wre_ref.dtype)
    wnim_ref[...] = (-(pw1_re * cti + pw1_im * ctr)).astype(wnim_ref.dtype)
    kdir = []
    for d in range(2):
        kd = (jnp.einsum("gop,gpn->gon", cr_ref[d], v_re[d], precision=lax.Precision.HIGHEST,
                         preferred_element_type=F32)
              - jnp.einsum("gop,gpn->gon", ci_ref[d], v_im[d], precision=lax.Precision.HIGHEST,
                           preferred_element_type=F32))
        kdir.append(kd)
    lag_k = lax.broadcasted_iota(jnp.int32, kdir[0].shape, 2) // ch
    kall_ref[0] = (kdir[0] + jnp.where(lag_k == 0, kdir[1], 0.0)).astype(kall_ref.dtype)
    kall_ref[1] = kdir[1].astype(kall_ref.dtype)


def s5_operators(lam_re, lam_im, log_dt, b_re, b_im, c_re, c_im, *, gb):
    _, g, p = lam_re.shape
    ch = b_re.shape[-1]
    n = S5_CHUNK * ch
    col = lambda a: a.reshape(2, g, p, 1)
    tile = lambda a: jnp.tile(a, (1, 1, 1, S5_CHUNK))
    ct = lambda a: tile(jnp.swapaxes(a, 2, 3))
    spec = lambda *tail: pl.BlockSpec((2, gb) + tail, lambda i: (0, i, 0, 0))
    shp = lambda *tail, dtype=F32: jax.ShapeDtypeStruct((2, g) + tail, dtype)
    big = functools.partial(shp, dtype=BF16)
    vre, vim, kall, wre, wnim, are, aim = pl.pallas_call(
        functools.partial(_s5_prep_kernel, chunk=S5_CHUNK, ch=ch),
        grid=(g // gb,),
        in_specs=[spec(p, 1), spec(p, 1), spec(1, 1), spec(p, n), spec(p, n), spec(p, n), spec(p, n),
                  spec(ch, p), spec(ch, p)],
        out_specs=[spec(p, n), spec(p, n), spec(ch, n), spec(p, n), spec(p, n), spec(p, 1), spec(p, 1)],
        out_shape=[big(p, n), big(p, n), big(ch, n), big(p, n), big(p, n), shp(p, 1), shp(p, 1)],
        compiler_params=_params(("parallel",)),
        name="s5_prep",
    )(col(lam_re), col(lam_im), log_dt.reshape(2, g, 1, 1), tile(b_re), tile(b_im), ct(c_re), ct(c_im),
      c_re, c_im)

    L = S5_CHUNK
    s_idx = jnp.arange(L)[:, None]
    t_idx = jnp.arange(L)[None, :]
    k5 = kall.reshape(2, g, ch, L, ch)
    kf = jnp.take(k5[0], jnp.abs(t_idx - s_idx).reshape(-1), axis=2).reshape(g, ch, L, L, ch)
    kb = jnp.take(k5[1], jnp.abs(t_idx - s_idx).reshape(-1), axis=2).reshape(g, ch, L, L, ch)
    sel = (t_idx >= s_idx)[None, None, :, :, None]
    t_op = jnp.where(sel, kf, kb).transpose(0, 2, 4, 3, 1).reshape(g, n, n)

    v5 = lambda a: a.reshape(2, g, p, L, ch)
    e_f = lambda a: jnp.flip(v5(a)[0], axis=2).transpose(0, 2, 3, 1).reshape(g, n, p)
    e_b = lambda a: v5(a)[1].transpose(0, 2, 3, 1).reshape(g, n, p)
    e_op = jnp.concatenate([e_f(vre), e_b(vre), e_f(vim), e_b(vim)], axis=-1)

    w5 = lambda a: a.reshape(2, g, p, L, ch)
    f_f = lambda a: w5(a)[0].reshape(g, p, n)
    f_b = lambda a: jnp.flip(w5(a)[1], axis=2).reshape(g, p, n)
    f_op = jnp.concatenate([f_f(wre), f_b(wre), f_f(wnim), f_b(wnim)], axis=1)

    lanes = lambda a: jnp.concatenate([a[0, :, :, 0], a[1, :, :, 0]], axis=-1).reshape(g, 1, 2 * p)
    return t_op, e_op, f_op, lanes(are), lanes(aim)


def _s5_group_kernel(x_ref, t_ref, e_ref, f_ref, are_ref, aim_ref, o_ref, s_ref, c_ref, *, nb, n_chunks):
    x = x_ref[0]
    half = e_ref.shape[2] // 2
    s_ref[...] = jnp.dot(x, e_ref[0], preferred_element_type=F32)
    a_re = jnp.broadcast_to(are_ref[0], (nb, half))
    a_im = jnp.broadcast_to(aim_ref[0], (nb, half))
    fwd = lax.broadcasted_iota(jnp.int32, (nb, half), 1) < half // 2

    def step(kk, carry):
        h_re, h_im = carry
        rf = pl.multiple_of(kk * nb, nb)
        rb = pl.multiple_of((n_chunks - 1 - kk) * nb, nb)
        c_ref[pl.ds(rf, nb), 0:half // 2] = h_re[:, 0:half // 2]
        c_ref[pl.ds(rb, nb), half // 2:half] = h_re[:, half // 2:half]
        c_ref[pl.ds(rf, nb), half:half + half // 2] = h_im[:, 0:half // 2]
        c_ref[pl.ds(rb, nb), half + half // 2:] = h_im[:, half // 2:half]
        s_re = jnp.where(fwd, s_ref[pl.ds(rf, nb), 0:half], s_ref[pl.ds(rb, nb), 0:half])
        s_im = jnp.where(fwd, s_ref[pl.ds(rf, nb), half:], s_ref[pl.ds(rb, nb), half:])
        return (a_re * h_re - a_im * h_im + s_re, a_re * h_im + a_im * h_re + s_im)

    zero = jnp.zeros((nb, half), F32)
    lax.fori_loop(0, n_chunks, step, (zero, zero))
    y = jnp.dot(x, t_ref[0], preferred_element_type=F32)
    y = y + jnp.dot(c_ref[...].astype(x.dtype), f_ref[0], preferred_element_type=F32)
    o_ref[0] = y.astype(o_ref.dtype)


def s5_scan(xg, t_op, e_op, f_op, a_re, a_im, *, nb):
    g, rows, n = xg.shape
    grp = lambda *tail: pl.BlockSpec((1,) + tail, lambda i: (i, 0, 0))
    return pl.pallas_call(
        functools.partial(_s5_group_kernel, nb=nb, n_chunks=rows // nb),
        grid=(g,),
        in_specs=[grp(rows, n), grp(n, n), grp(n, n), grp(n, n), grp(1, n // 2), grp(1, n // 2)],
        out_specs=grp(rows, n),
        out_shape=jax.ShapeDtypeStruct((g, rows, n), F32),
        scratch_shapes=[pltpu.VMEM((rows, n), F32), pltpu.VMEM((rows, n), F32)],
        compiler_params=_params(("parallel",)),
        name="s5_scan",
    )(xg, t_op, e_op, f_op, a_re, a_im)


LANES = 128


def _grid_transposes(sets, idx, axis, unit):
    n = len(sets[0])
    size = n * unit
    sets = [list(v) for v in sets]
    d = 1
    while d < n:
        low = (idx & d) == 0
        for v in sets:
            for j in range(n):
                if j & d == 0:
                    a, b = v[j], v[j + d]
                    v[j] = jnp.where(low, a, pltpu.roll(b, d * unit, axis))
                    v[j + d] = jnp.where(low, pltpu.roll(a, size - d * unit, axis), b)
        d *= 2
    return sets


def _inv_rms_to(inv_ref, x_ref):
    x = x_ref[...]
    inv = lax.rsqrt(jnp.mean(x * x, axis=-1, keepdims=True) + RMS_EPS)
    inv_ref[...] = jnp.broadcast_to(inv, inv_ref.shape)


def _s5_in_kernel(x_ref, g_ref, o_ref, inv_ref, *, tb, ch):
    nb, _, d = x_ref.shape
    per = LANES // ch
    assert per == SUBLANES and nb % SUBLANES == 0 and S5_CHUNK % SUBLANES == 0
    _inv_rms_to(inv_ref, x_ref)
    piece = lax.broadcasted_iota(jnp.int32, (nb, LANES), 1) // ch
    sub = lax.broadcasted_iota(jnp.int32, (SUBLANES, LANES), 0)

    def tile_body(lt, carry):
        l0 = pl.multiple_of(lt * LANES, LANES)
        gl = g_ref[:, pl.ds(l0, LANES)]
        n_bb, n_t8 = nb // SUBLANES, S5_CHUNK // SUBLANES
        keys = [(c, bb, t8) for c in range(tb) for bb in range(n_bb) for t8 in range(n_t8)]
        sets = []
        for c, bb, t8 in keys:
            rows = slice(c * S5_CHUNK + t8 * SUBLANES, c * S5_CHUNK + (t8 + 1) * SUBLANES)
            sets.append([x_ref[bb * SUBLANES + b, rows, pl.ds(l0, LANES)] * inv_ref[bb * SUBLANES + b, rows, :] * gl
                         for b in range(SUBLANES)])
        by_step = dict(zip(keys, _grid_transposes(sets, sub, 0, 1)))
        keys = [(c, h) for c in range(tb) for h in range(S5_CHUNK // per)]
        sets = [[jnp.concatenate([by_step[c, bb, (h * per + s) // SUBLANES][(h * per + s) % SUBLANES]
                                  for bb in range(n_bb)], axis=0) for s in range(per)] for c, h in keys]
        for (c, h), w in zip(keys, _grid_transposes(sets, piece, 1, ch)):
            for gg in range(per):
                o_ref[lt * per + gg, c * nb:(c + 1) * nb, h * LANES:(h + 1) * LANES] = w[gg].astype(o_ref.dtype)
        return carry

    lax.fori_loop(0, d // LANES, tile_body, 0)


def s5_to_groups(x3, g, *, ch, tb):
    nb, s, d = x3.shape
    rows = tb * S5_CHUNK
    return pl.pallas_call(
        functools.partial(_s5_in_kernel, tb=tb, ch=ch),
        grid=(s // rows,),
        in_specs=[pl.BlockSpec((nb, rows, d), lambda i: (0, i, 0)), pl.BlockSpec((1, d), lambda i: (0, 0))],
        out_specs=pl.BlockSpec((d // ch, tb * nb, S5_CHUNK * ch), lambda i: (0, i, 0)),
        out_shape=jax.ShapeDtypeStruct((d // ch, (s // S5_CHUNK) * nb, S5_CHUNK * ch), BF16),
        scratch_shapes=[pltpu.VMEM((nb, rows, LANES), F32)],
        compiler_params=_params(("parallel",)),
        name="s5_to_groups",
    )(x3, g.reshape(1, d))


def _s5_out_kernel(yg_ref, x_ref, g_ref, d_ref, o_ref, inv_ref, *, tb, ch):
    nb, _, d = x_ref.shape
    per = LANES // ch
    assert per == SUBLANES and nb % SUBLANES == 0 and S5_CHUNK % SUBLANES == 0
    _inv_rms_to(inv_ref, x_ref)
    piece = lax.broadcasted_iota(jnp.int32, (nb, LANES), 1) // ch
    sub = lax.broadcasted_iota(jnp.int32, (SUBLANES, LANES), 0)

    def tile_body(lt, carry):
        l0 = pl.multiple_of(lt * LANES, LANES)
        gl = g_ref[:, pl.ds(l0, LANES)]
        dl = d_ref[:, pl.ds(l0, LANES)]
        n_bb, n_t8 = nb // SUBLANES, S5_CHUNK // SUBLANES
        keys = [(c, h) for c in range(tb) for h in range(S5_CHUNK // per)]
        sets = [[yg_ref[lt * per + gg, c * nb:(c + 1) * nb, h * LANES:(h + 1) * LANES] for gg in range(per)]
                for c, h in keys]
        by_step = dict(zip(keys, _grid_transposes(sets, piece, 1, ch)))
        keys = [(c, bb, t8) for c in range(tb) for bb in range(n_bb) for t8 in range(n_t8)]
        sets = [[by_step[c, (t8 * SUBLANES + t) // per][(t8 * SUBLANES + t) % per][bb * SUBLANES:(bb + 1) * SUBLANES, :]
                 for t in range(SUBLANES)] for c, bb, t8 in keys]
        by_batch = dict(zip(keys, _grid_transposes(sets, sub, 0, 1)))
        for c in range(tb):
            for bb in range(n_bb):
                for b in range(SUBLANES):
                    acts = []
                    for t8 in range(n_t8):
                        rows = slice(c * S5_CHUNK + t8 * SUBLANES, c * S5_CHUNK + (t8 + 1) * SUBLANES)
                        xn = x_ref[bb * SUBLANES + b, rows, pl.ds(l0, LANES)] * inv_ref[bb * SUBLANES + b, rows, :] * gl
                        u = by_batch[c, bb, t8][b] + dl * xn
                        acts.append(0.5 * u * (1.0 + jnp.tanh(math.sqrt(2.0 / math.pi) * (u + 0.044715 * (u * u * u)))))
                    o_ref[bb * SUBLANES + b, c * S5_CHUNK:(c + 1) * S5_CHUNK, pl.ds(l0, LANES)] = (
                        jnp.concatenate(acts, axis=0).astype(o_ref.dtype))
        return carry

    lax.fori_loop(0, d // LANES, tile_body, 0)


def s5_from_groups(yg, x3, g, d_skip, *, ch, tb):
    nb, s, d = x3.shape
    rows = tb * S5_CHUNK
    return pl.pallas_call(
        functools.partial(_s5_out_kernel, tb=tb, ch=ch),
        grid=(s // rows,),
        in_specs=[pl.BlockSpec((d // ch, tb * nb, S5_CHUNK * ch), lambda i: (0, i, 0)),
                  pl.BlockSpec((nb, rows, d), lambda i: (0, i, 0)),
                  pl.BlockSpec((1, d), lambda i: (0, 0)),
                  pl.BlockSpec((1, d), lambda i: (0, 0))],
        out_specs=pl.BlockSpec((nb, rows, d), lambda i: (0, i, 0)),
        out_shape=jax.ShapeDtypeStruct((nb, s, d), BF16),
        scratch_shapes=[pltpu.VMEM((nb, rows, LANES), F32)],
        compiler_params=_params(("parallel",)),
        name="s5_from_groups",
    )(yg, x3, g.reshape(1, d), d_skip.reshape(1, d))


def _s5_glu_kernel(a_ref, x_ref, wv_ref, wg_ref, o_ref):
    a = a_ref[...]
    val = jnp.dot(a, wv_ref[...], preferred_element_type=F32)
    gate = jnp.dot(a, wg_ref[...], preferred_element_type=F32)
    o_ref[...] = x_ref[...] + val * jax.nn.sigmoid(gate)


def s5_glu(act, x, w_val, w_gate, *, tm):
    m, d = x.shape
    return pl.pallas_call(
        _s5_glu_kernel,
        grid=(m // tm,),
        in_specs=[pl.BlockSpec((tm, d), lambda i: (i, 0)),
                  pl.BlockSpec((tm, d), lambda i: (i, 0)),
                  _resident(w_val.shape), _resident(w_gate.shape)],
        out_specs=pl.BlockSpec((tm, d), lambda i: (i, 0)),
        out_shape=jax.ShapeDtypeStruct((m, d), F32),
        compiler_params=_params(("parallel",)),
        name="s5_glu",
    )(act, x, w_val, w_gate)


def _mixer_conv_diff(x, norm_g, w_in, conv_dw, conv_db, conv_ln_g, conv_ln_b, lq1, lk1, lq2, lk2, subln_g, w_out,
                     *, batch, seq, lambda_init):
    d_conv = conv_dw.shape[1]
    d_diff = (w_in.shape[1] - 2 * d_conv) // 3
    z = norm_matmul(x, norm_g, w_in, tm=1024, tn=1280)
    a_out = conformer_conv(z, conv_dw, conv_db, conv_ln_g, conv_ln_b, seq=seq, tr=256)
    b_out = diff_attention(z, lq1, lk1, lq2, lk2, subln_g, batch=batch, seq=seq,
                           q_col=2 * d_conv, k_col=2 * d_conv + d_diff, v_col=2 * d_conv + 2 * d_diff,
                           lambda_init=lambda_init, tq=512)
    return matmul_residual([a_out, b_out], w_out, x, tm=512)


def _mixer_s5(x, norm_g, lam_re, lam_im, log_dt, b_re, b_im, c_re, c_im, d_skip, w_val, w_gate, *, batch, seq):
    m, d = x.shape
    ch = b_re.shape[-1]
    t_op, e_op, f_op, a_re, a_im = s5_operators(lam_re, lam_im, log_dt, b_re, b_im, c_re, c_im, gb=8)
    x3 = x.reshape(batch, seq, d)
    xg = s5_to_groups(x3, norm_g, ch=ch, tb=4)
    yg = s5_scan(xg, t_op, e_op, f_op, a_re, a_im, nb=batch)
    act = s5_from_groups(yg, x3, norm_g, d_skip, ch=ch, tb=4).reshape(m, d)
    return s5_glu(act, x, w_val, w_gate, tm=512)


def _cross_attn_block(x, memf, norm_g, mem_g, wq, wk, wv, wo, *, batch, seq, mem_len):
    d = x.shape[1]
    q = norm_matmul(x, norm_g, wq, tm=512, tn=d)
    kv = norm_matmul(memf, mem_g, jnp.concatenate([wk, wv], axis=1), tm=1024, tn=1024)
    o = cross_attention(q, kv, batch=batch, seq=seq, mem_len=mem_len, tq=512)
    return matmul_residual([o], wo, x, tm=512)


def kernel(x, mem, norm_mix_g, ab_w_in, conv_dw, conv_db, conv_ln_g, conv_ln_b, diff_lq1, diff_lk1, diff_lq2, diff_lk2, diff_subln_g, ab_w_out, s5_lam_re, s5_lam_im, s5_log_dt, s5_b_re, s5_b_im, s5_c_re, s5_c_im, s5_d, s5_w_val, s5_w_gate, norm_xa_g, norm_mem_g, xa_wq, xa_wk, xa_wv, xa_wo, norm_ffn_g, ffn_w_up, ffn_dw, ffn_db, ffn_w_down, final_g):
    batch, seq, d = x.shape
    mem_len = mem.shape[1]
    depth = norm_mix_g.shape[0]
    xf = x.reshape(batch * seq, d)
    memf = mem.reshape(batch * mem_len, d)
    for layer in range(depth):
        i = layer // 2
        if layer % 2 == 0:
            lambda_init = 0.8 - 0.6 * math.exp(-0.3 * layer)
            xf = _mixer_conv_diff(xf, norm_mix_g[layer], layer_bf16(ab_w_in, i), conv_dw[i], conv_db[i], conv_ln_g[i],
                                  conv_ln_b[i], diff_lq1[i], diff_lk1[i], diff_lq2[i], diff_lk2[i],
                                  diff_subln_g[i], layer_bf16(ab_w_out, i), batch=batch, seq=seq,
                                  lambda_init=lambda_init)
        else:
            xf = _mixer_s5(xf, norm_mix_g[layer], s5_lam_re[i], s5_lam_im[i], s5_log_dt[i], s5_b_re[i],
                           s5_b_im[i], s5_c_re[i], s5_c_im[i], s5_d[i], layer_bf16(s5_w_val, i),
                           layer_bf16(s5_w_gate, i), batch=batch, seq=seq)
        xf = _cross_attn_block(xf, memf, norm_xa_g[layer], norm_mem_g[layer], layer_bf16(xa_wq, layer),
                               layer_bf16(xa_wk, layer), layer_bf16(xa_wv, layer), layer_bf16(xa_wo, layer),
                               batch=batch, seq=seq, mem_len=mem_len)
        xf = conv_ffn_block(xf, norm_ffn_g[layer], layer_bf16(ffn_w_up, layer), ffn_dw[layer], ffn_db[layer],
                            layer_bf16(ffn_w_down, layer), final_g, seq=seq, tm=512, tf=512,
                            final_norm=(layer == depth - 1))
    return xf.reshape(batch, seq, d)
```

```python
import functools
import math

import jax
import jax.numpy as jnp
from jax import lax
from jax.experimental import pallas as pl
from jax.experimental.pallas import tpu as pltpu

RMS_EPS = 1e-6
LN_EPS = 1e-5
XA_HEADS = 4
S5_CHUNK = 16
V7X_VMEM_BUDGET = 56 * 1024 * 1024
BF16 = jnp.bfloat16
F32 = jnp.float32


def _params(semantics, vmem=V7X_VMEM_BUDGET):
    return pltpu.CompilerParams(dimension_semantics=semantics, vmem_limit_bytes=vmem)


def _resident(shape):
    return pl.BlockSpec(shape, lambda *_: (0,) * len(shape), pipeline_mode=pl.Buffered(1))


def _rms(x, g, eps):
    return x * lax.rsqrt(jnp.mean(x * x, axis=-1, keepdims=True) + eps) * g


CAST_BLOCK_ELEMS = 1 << 20


def _cast_kernel(w_ref, o_ref):
    o_ref[...] = w_ref[...].astype(o_ref.dtype)


def layer_bf16(w, idx):
    _, k, n = w.shape
    tk = 16
    while tk * 2 * n <= CAST_BLOCK_ELEMS and k % (tk * 2) == 0:
        tk *= 2
    return pl.pallas_call(
        _cast_kernel,
        grid=(k // tk,),
        in_specs=[pl.BlockSpec((None, tk, n), lambda i: (idx, i, 0))],
        out_specs=pl.BlockSpec((tk, n), lambda i: (i, 0)),
        out_shape=jax.ShapeDtypeStruct((k, n), BF16),
        compiler_params=_params(("parallel",)),
        name="layer_bf16",
    )(w)


def _norm_matmul_kernel(x_ref, g_ref, w_ref, o_ref, xn_ref):
    @pl.when(pl.program_id(1) == 0)
    def _():
        xn_ref[...] = _rms(x_ref[...], g_ref[...], RMS_EPS).astype(xn_ref.dtype)

    o_ref[...] = jnp.dot(xn_ref[...], w_ref[...], preferred_element_type=F32).astype(o_ref.dtype)


def norm_matmul(x, g, w, *, tm, tn):
    m, k = x.shape
    n = w.shape[1]
    w_spec = _resident((k, n)) if tn == n else pl.BlockSpec((k, tn), lambda i, j: (0, j))
    return pl.pallas_call(
        _norm_matmul_kernel,
        grid=(m // tm, n // tn),
        in_specs=[pl.BlockSpec((tm, k), lambda i, j: (i, 0)),
                  pl.BlockSpec((1, k), lambda i, j: (0, 0)),
                  w_spec],
        out_specs=pl.BlockSpec((tm, tn), lambda i, j: (i, j)),
        out_shape=jax.ShapeDtypeStruct((m, n), BF16),
        scratch_shapes=[pltpu.VMEM((tm, k), BF16)],
        compiler_params=_params(("parallel", "arbitrary")),
        name="norm_matmul",
    )(x, g.reshape(1, k), w)


def _matmul_residual_kernel(*refs, n_lhs):
    a_refs, w_refs = refs[:n_lhs], refs[n_lhs:2 * n_lhs]
    r_ref, o_ref = refs[2 * n_lhs], refs[2 * n_lhs + 1]
    acc = r_ref[...]
    for a_ref, w_ref in zip(a_refs, w_refs):
        acc = acc + jnp.dot(a_ref[...], w_ref[...], preferred_element_type=F32)
    o_ref[...] = acc


def matmul_residual(lhs, w, res, *, tm):
    m, n = res.shape
    kb = w.shape[0] // len(lhs)
    assert all(a.shape[1] == kb for a in lhs)
    in_specs = [pl.BlockSpec((tm, kb), lambda i: (i, 0)) for _ in lhs]
    in_specs += [pl.BlockSpec((kb, n), lambda i, r=r: (r, 0), pipeline_mode=pl.Buffered(1)) for r in range(len(lhs))]
    in_specs += [pl.BlockSpec((tm, n), lambda i: (i, 0))]
    return pl.pallas_call(
        functools.partial(_matmul_residual_kernel, n_lhs=len(lhs)),
        grid=(m // tm,),
        in_specs=in_specs,
        out_specs=pl.BlockSpec((tm, n), lambda i: (i, 0)),
        out_shape=jax.ShapeDtypeStruct((m, n), F32),
        compiler_params=_params(("parallel",)),
        name="matmul_residual",
    )(*lhs, *([w] * len(lhs)), res)


CONV_HALO = 16
SUBLANES = 8


def _conformer_kernel(vp_ref, gp_ref, v_ref, g_ref, vn_ref, gn_ref, dw_ref, db_ref, lg_ref, lb_ref,
                      o_ref, h_ref, *, tr, seq, width, sub):
    i = pl.program_id(0)
    half = width // 2
    has_prev = (i * tr) % seq != 0
    has_next = ((i + 1) * tr) % seq != 0
    rows = tr + 2 * CONV_HALO

    def glu(v, g):
        return v.astype(F32) * jax.nn.sigmoid(g.astype(F32))

    h_ref[0, 0:CONV_HALO, :] = jnp.where(has_prev, glu(vp_ref[...], gp_ref[...]), 0.0)
    h_ref[0, CONV_HALO:CONV_HALO + tr, :] = glu(v_ref[...], g_ref[...])
    h_ref[0, CONV_HALO + tr:, :] = jnp.where(has_next, glu(vn_ref[...], gn_ref[...]), 0.0)
    for b in range(1, SUBLANES):
        h_ref[b, 0:rows - SUBLANES, :] = h_ref[0, b:b + rows - SUBLANES, :]

    def body(r, carry):
        r0 = pl.multiple_of(r * sub, sub)
        acc = jnp.zeros((sub, h_ref.shape[2]), F32) + db_ref[...]
        for k in range(width):
            a, b = divmod(CONV_HALO - half + k, SUBLANES)
            acc = acc + dw_ref[k:k + 1, :] * h_ref[b, pl.ds(r0 + a * SUBLANES, sub), :]
        mu = jnp.mean(acc, axis=-1, keepdims=True)
        cen = acc - mu
        var = jnp.mean(cen * cen, axis=-1, keepdims=True)
        y = cen * lax.rsqrt(var + LN_EPS) * lg_ref[...] + lb_ref[...]
        o_ref[pl.ds(r0, sub), :] = (y * jax.nn.sigmoid(y)).astype(o_ref.dtype)
        return carry

    lax.fori_loop(0, tr // sub, body, 0, unroll=8)


def conformer_conv(z, dw, db, ln_g, ln_b, *, seq, tr):
    m = z.shape[0]
    width, c = dw.shape
    hb = tr // CONV_HALO
    last = m // CONV_HALO - 1
    prev_map = lambda col: (lambda i: (jnp.maximum(i * hb - 1, 0), col))
    next_map = lambda col: (lambda i: (jnp.minimum((i + 1) * hb, last), col))
    main_map = lambda col: (lambda i: (i, col))
    row = lambda a: a.reshape(1, c)
    full = lambda shape: pl.BlockSpec(shape, lambda i: (0, 0))
    return pl.pallas_call(
        functools.partial(_conformer_kernel, tr=tr, seq=seq, width=width, sub=32),
        grid=(m // tr,),
        in_specs=[pl.BlockSpec((CONV_HALO, c), prev_map(0)), pl.BlockSpec((CONV_HALO, c), prev_map(1)),
                  pl.BlockSpec((tr, c), main_map(0)), pl.BlockSpec((tr, c), main_map(1)),
                  pl.BlockSpec((CONV_HALO, c), next_map(0)), pl.BlockSpec((CONV_HALO, c), next_map(1)),
                  full((width, c)), full((1, c)), full((1, c)), full((1, c))],
        out_specs=pl.BlockSpec((tr, c), lambda i: (i, 0)),
        out_shape=jax.ShapeDtypeStruct((m, c), BF16),
        scratch_shapes=[pltpu.VMEM((SUBLANES, tr + 2 * CONV_HALO, c), F32)],
        compiler_params=_params(("parallel",)),
        name="conformer_conv",
    )(z, z, z, z, z, z, dw, row(db), row(ln_g), row(ln_b))


def _diff_attn_kernel(slopes_ref, q_ref, k_ref, v_ref, lq1_ref, lk1_ref, lq2_ref, lk2_ref, sg_ref, o_ref,
                      bias_ref, s0_ref, s1_ref, w0_ref, w1_ref, *, tq, dh, n_tiles, tiles_per_head, nq, pieces, lambda_init):
    g = pl.program_id(0)
    s_len = k_ref.shape[0]
    tile_b = jnp.clip(g - 1, 0, n_tiles - 1)
    s_ref = (s0_ref, s1_ref)
    w_ref = (w0_ref, w1_ref)

    @pl.when(g == 0)
    def _():
        s1_ref[...] = jnp.zeros(s1_ref.shape, s1_ref.dtype)
        w0_ref[...] = jnp.zeros(w0_ref.shape, w0_ref.dtype)

    @pl.when((tile_b % tiles_per_head == 0) & (g <= n_tiles))
    def _():
        row = lax.broadcasted_iota(jnp.int32, bias_ref.shape, 0)
        col = lax.broadcasted_iota(jnp.int32, bias_ref.shape, 1)
        bias_ref[...] = -slopes_ref[tile_b // tiles_per_head] * jnp.abs(row + (s_len - tq) - col).astype(F32)

    def stages(slot_a, slot_b):
        q = q_ref[...]
        lane = lax.broadcasted_iota(jnp.int32, q.shape, 1)
        scale = jnp.asarray(dh ** -0.5, q.dtype)
        qq = jnp.concatenate([jnp.where(mask, q, 0) * scale for mask in (lane < dh, lane >= dh)], axis=0)
        nt = (((1,), (1,)), ((), ()))
        lam = (jnp.exp(jnp.sum(lq1_ref[...] * lk1_ref[...], axis=-1, keepdims=True))
               - jnp.exp(jnp.sum(lq2_ref[...] * lk2_ref[...], axis=-1, keepdims=True)) + lambda_init)
        start = pl.multiple_of((s_len - tq) - (tile_b % nq) * tq, tq)
        kb = s_len // pieces
        rb = tq // pieces
        o = jnp.zeros(o_ref.shape, F32)
        for j in range(pieces):
            keys = slice(j * kb, (j + 1) * kb)
            rows = slice(j * rb, (j + 1) * rb)
            s_both = lax.dot_general(qq, k_ref[keys, :], nt, preferred_element_type=F32)
            s_ref[slot_a][0, :, keys] = s_both[:tq]
            s_ref[slot_a][1, :, keys] = s_both[tq:]
            bias = bias_ref[rows, pl.ds(start, s_len)]

            def softmax_parts(s):
                s = s + bias
                e = jnp.exp(s - jnp.max(s, axis=-1, keepdims=True))
                return e, jnp.sum(e, axis=-1, keepdims=True)

            e1, l1 = softmax_parts(s_ref[slot_b][0, rows, :])
            e2, l2 = softmax_parts(s_ref[slot_b][1, rows, :])
            w_ref[slot_b][rows, :] = (e1 * (1.0 / l1) - e2 * (lam / l2)).astype(w0_ref.dtype)
            o = o + jnp.dot(w_ref[slot_a][:, keys], v_ref[keys, :], preferred_element_type=F32)
        o_ref[...] = (_rms(o, sg_ref[...], LN_EPS) * (1.0 - lambda_init)).astype(o_ref.dtype)

    pl.when(g % 2 == 0)(lambda: stages(0, 1))
    pl.when(g % 2 == 1)(lambda: stages(1, 0))


def diff_attention(z, lq1, lk1, lq2, lk2, subln_g, *, batch, seq, q_col, k_col, v_col, lambda_init, tq):
    dh = lq1.shape[-1]
    dv = subln_g.shape[-1]
    heads = (k_col - q_col) // (2 * dh)
    nq = seq // tq
    qb, kb, vb = q_col // dv, k_col // dv, v_col // dv
    n_tiles = heads * batch * nq
    slopes = jnp.exp2(-8.0 * jnp.arange(1, heads + 1, dtype=F32) / heads)

    def tile(g, lag):
        t = jnp.clip(g - lag, 0, n_tiles - 1)
        return t // (batch * nq), (t // nq) % batch, t % nq

    def q_map(g):
        h, b, i = tile(g, 0)
        return b * nq + i, qb + h

    def k_map(g):
        h, b, _ = tile(g, 0)
        return b, kb + h

    def v_map(g):
        h, b, _ = tile(g, 2)
        return b, vb + h

    def o_map(g):
        h, b, i = tile(g, 2)
        return b * nq + i, h

    vec = lambda a: a.reshape(1, -1)
    small = lambda n: pl.BlockSpec((1, n), lambda g: (0, 0))
    return pl.pallas_call(
        functools.partial(_diff_attn_kernel, tq=tq, dh=dh, n_tiles=n_tiles, tiles_per_head=batch * nq, nq=nq, pieces=4,
                          lambda_init=lambda_init),
        grid=(n_tiles + 2,),
        in_specs=[pl.BlockSpec(memory_space=pltpu.SMEM),
                  pl.BlockSpec((tq, 2 * dh), q_map),
                  pl.BlockSpec((seq, 2 * dh), k_map),
                  pl.BlockSpec((seq, dv), v_map),
                  small(dh), small(dh), small(dh), small(dh), small(dv)],
        out_specs=pl.BlockSpec((tq, dv), o_map),
        out_shape=jax.ShapeDtypeStruct((batch * seq, heads * dv), BF16),
        scratch_shapes=[pltpu.VMEM((tq, 2 * seq - tq), F32),
                        pltpu.VMEM((2, tq, seq), F32), pltpu.VMEM((2, tq, seq), F32),
                        pltpu.VMEM((tq, seq), BF16), pltpu.VMEM((tq, seq), BF16)],
        compiler_params=_params(("arbitrary",)),
        name="diff_attention",
    )(slopes, z, z, z, vec(lq1), vec(lk1), vec(lq2), vec(lk2), vec(subln_g))


def _cross_attn_kernel(q_ref, k_ref, v_ref, o_ref, *, heads):
    dh = q_ref.shape[1] // heads
    nt = (((1,), (1,)), ((), ()))
    cols = [slice(h * dh, (h + 1) * dh) for h in range(heads)]
    s = [lax.dot_general(q_ref[:, c], k_ref[:, c], nt, preferred_element_type=F32) * (dh ** -0.5) for c in cols]
    m = [jnp.max(x, axis=-1, keepdims=True) for x in s]
    e = [jnp.exp(x - mx) for x, mx in zip(s, m)]
    l = [jnp.sum(x, axis=-1, keepdims=True) for x in e]
    p = [(x / lx).astype(v_ref.dtype) for x, lx in zip(e, l)]
    for c, ph in zip(cols, p):
        o_ref[:, c] = jnp.dot(ph, v_ref[:, c], preferred_element_type=F32).astype(o_ref.dtype)


def cross_attention(q, kv, *, batch, seq, mem_len, tq):
    d = q.shape[1]
    nq = seq // tq
    return pl.pallas_call(
        functools.partial(_cross_attn_kernel, heads=XA_HEADS),
        grid=(batch, nq),
        in_specs=[pl.BlockSpec((tq, d), lambda b, i: (b * nq + i, 0)),
                  pl.BlockSpec((mem_len, d), lambda b, i: (b, 0)),
                  pl.BlockSpec((mem_len, d), lambda b, i: (b, 1))],
        out_specs=pl.BlockSpec((tq, d), lambda b, i: (b * nq + i, 0)),
        out_shape=jax.ShapeDtypeStruct((batch * seq, d), BF16),
        compiler_params=_params(("parallel", "arbitrary")),
        name="cross_attention",
    )(q, kv, kv)


FFN_HALO = 8


def _ffn_kernel(xp_ref, x_ref, xn_ref, g_ref, wg_ref, wv_ref, dwg_ref, dwv_ref, dbg_ref, dbv_ref, wd_ref,
                fg_ref, o_ref, xs_ref, hg_ref, hv_ref, *, tm, seq, final_norm):
    i = pl.program_id(0)
    f = pl.program_id(1)

    @pl.when(f == 0)
    def _():
        has_prev = (i * tm) % seq != 0
        has_next = ((i + 1) * tm) % seq != 0
        norm = lambda x: _rms(x, g_ref[...], RMS_EPS)
        xs_ref[0:FFN_HALO, :] = jnp.where(has_prev, norm(xp_ref[...]), 0.0).astype(xs_ref.dtype)
        xs_ref[FFN_HALO:FFN_HALO + tm, :] = norm(x_ref[...]).astype(xs_ref.dtype)
        xs_ref[FFN_HALO + tm:, :] = jnp.where(has_next, norm(xn_ref[...]), 0.0).astype(xs_ref.dtype)
        o_ref[...] = x_ref[...]

    xs = xs_ref[...]
    hg_ref[...] = jnp.dot(xs, wg_ref[...], preferred_element_type=F32)
    hv_ref[...] = jnp.dot(xs, wv_ref[...], preferred_element_type=F32)

    def conv(h_ref, dw_ref, db_ref):
        out = db_ref[...] + dw_ref[0:1, :] * h_ref[FFN_HALO - 1:FFN_HALO - 1 + tm, :]
        out = out + dw_ref[1:2, :] * h_ref[FFN_HALO:FFN_HALO + tm, :]
        return out + dw_ref[2:3, :] * h_ref[FFN_HALO + 1:FFN_HALO + 1 + tm, :]

    gate = conv(hg_ref, dwg_ref, dbg_ref)
    act = gate * jax.nn.sigmoid(gate) * conv(hv_ref, dwv_ref, dbv_ref)
    o_ref[...] += jnp.dot(act.astype(wd_ref.dtype), wd_ref[...], preferred_element_type=F32)

    if final_norm:
        @pl.when(f == pl.num_programs(1) - 1)
        def _():
            o_ref[...] = _rms(o_ref[...], fg_ref[...], RMS_EPS)


def conv_ffn_block(x, g, w_up, dw, db, w_down, final_g, *, seq, tm, tf, final_norm):
    m, d = x.shape
    d_ff = w_down.shape[0]
    nf = d_ff // tf
    hb = tm // FFN_HALO
    last = m // FFN_HALO - 1
    db2 = db.reshape(1, 2 * d_ff)
    return pl.pallas_call(
        functools.partial(_ffn_kernel, tm=tm, seq=seq, final_norm=final_norm),
        grid=(m // tm, nf),
        in_specs=[pl.BlockSpec((FFN_HALO, d), lambda i, f: (jnp.maximum(i * hb - 1, 0), 0)),
                  pl.BlockSpec((tm, d), lambda i, f: (i, 0)),
                  pl.BlockSpec((FFN_HALO, d), lambda i, f: (jnp.minimum((i + 1) * hb, last), 0)),
                  pl.BlockSpec((1, d), lambda i, f: (0, 0)),
                  pl.BlockSpec((d, tf), lambda i, f: (0, f)),
                  pl.BlockSpec((d, tf), lambda i, f: (0, nf + f)),
                  pl.BlockSpec((dw.shape[0], tf), lambda i, f: (0, f)),
                  pl.BlockSpec((dw.shape[0], tf), lambda i, f: (0, nf + f)),
                  pl.BlockSpec((1, tf), lambda i, f: (0, f)),
                  pl.BlockSpec((1, tf), lambda i, f: (0, nf + f)),
                  pl.BlockSpec((tf, d), lambda i, f: (f, 0)),
                  pl.BlockSpec((1, d), lambda i, f: (0, 0))],
        out_specs=pl.BlockSpec((tm, d), lambda i, f: (i, 0)),
        out_shape=jax.ShapeDtypeStruct((m, d), F32),
        scratch_shapes=[pltpu.VMEM((tm + 2 * FFN_HALO, d), BF16),
                        pltpu.VMEM((tm + 2 * FFN_HALO, tf), F32),
                        pltpu.VMEM((tm + 2 * FFN_HALO, tf), F32)],
        compiler_params=_params(("parallel", "arbitrary")),
        name="conv_ffn",
    )(x, x, x, g.reshape(1, d), w_up, w_up, dw, dw, db2, db2, w_down, final_g.reshape(1, d))


def _s5_prep_kernel(lr_ref, li_ref, ldt_ref, btr_ref, bti_ref, ctr_ref, cti_ref, cr_ref, ci_ref,
                    vre_ref, vim_ref, kall_ref, wre_ref, wnim_ref, are_ref, aim_ref, *, chunk, ch):
    lr, li = lr_ref[...], li_ref[...]
    dt = jnp.exp(ldt_ref[...])
    mag = jnp.exp(lr * dt)
    lb_re, lb_im = mag * jnp.cos(li * dt), mag * jnp.sin(li * dt)
    den = lr * lr + li * li
    f_re = ((lb_re - 1.0) * lr + lb_im * li) / den
    f_im = (lb_im * lr - (lb_re - 1.0) * li) / den
    br, bi = btr_ref[...], bti_ref[...]
    bb_re = f_re * br - f_im * bi
    bb_im = f_re * bi + f_im * br
    lag = lax.broadcasted_iota(jnp.int32, br.shape, 3) // ch
    p_re, p_im = jnp.ones_like(lr), jnp.zeros_like(lr)
    pw_re, pw_im = jnp.zeros_like(br), jnp.zeros_like(br)
    for t in range(chunk):
        pw_re, pw_im = jnp.where(lag == t, p_re, pw_re), jnp.where(lag == t, p_im, pw_im)
        p_re, p_im = p_re * lb_re - p_im * lb_im, p_re * lb_im + p_im * lb_re
    are_ref[...], aim_ref[...] = p_re, p_im
    pw1_re, pw1_im = pw_re * lb_re - pw_im * lb_im, pw_re * lb_im + pw_im * lb_re
    v_re = pw_re * bb_re - pw_im * bb_im
    v_im = pw_re * bb_im + pw_im * bb_re
    vre_ref[...], vim_ref[...] = v_re.astype(vre_ref.dtype), v_im.astype(vim_ref.dtype)
    ctr, cti = ctr_ref[...], cti_ref[...]
    wre_ref[...] = (pw1_re * ctr - pw1_im * cti).astype(wre_ref.dtype)
    wnim_ref[...] = (-(pw1_re * cti + pw1_im * ctr)).astype(wnim_ref.dtype)
    kdir = []
    for d in range(2):
        kd = (jnp.einsum("gop,gpn->gon", cr_ref[d], v_re[d], precision=lax.Precision.HIGHEST,
                         preferred_element_type=F32)
              - jnp.einsum("gop,gpn->gon", ci_ref[d], v_im[d], precision=lax.Precision.HIGHEST,
                           preferred_element_type=F32))
        kdir.append(kd)
    lag_k = lax.broadcasted_iota(jnp.int32, kdir[0].shape, 2) // ch
    kall_ref[0] = (kdir[0] + jnp.where(lag_k == 0, kdir[1], 0.0)).astype(kall_ref.dtype)
    kall_ref[1] = kdir[1].astype(kall_ref.dtype)


def s5_operators(lam_re, lam_im, log_dt, b_re, b_im, c_re, c_im, *, gb):
    _, g, p = lam_re.shape
    ch = b_re.shape[-1]
    n = S5_CHUNK * ch
    col = lambda a: a.reshape(2, g, p, 1)
    tile = lambda a: jnp.tile(a, (1, 1, 1, S5_CHUNK))
    ct = lambda a: tile(jnp.swapaxes(a, 2, 3))
    spec = lambda *tail: pl.BlockSpec((2, gb) + tail, lambda i: (0, i, 0, 0))
    shp = lambda *tail, dtype=F32: jax.ShapeDtypeStruct((2, g) + tail, dtype)
    big = functools.partial(shp, dtype=BF16)
    vre, vim, kall, wre, wnim, are, aim = pl.pallas_call(
        functools.partial(_s5_prep_kernel, chunk=S5_CHUNK, ch=ch),
        grid=(g // gb,),
        in_specs=[spec(p, 1), spec(p, 1), spec(1, 1), spec(p, n), spec(p, n), spec(p, n), spec(p, n),
                  spec(ch, p), spec(ch, p)],
        out_specs=[spec(p, n), spec(p, n), spec(ch, n), spec(p, n), spec(p, n), spec(p, 1), spec(p, 1)],
        out_shape=[big(p, n), big(p, n), big(ch, n), big(p, n), big(p, n), shp(p, 1), shp(p, 1)],
        compiler_params=_params(("parallel",)),
        name="s5_prep",
    )(col(lam_re), col(lam_im), log_dt.reshape(2, g, 1, 1), tile(b_re), tile(b_im), ct(c_re), ct(c_im),
      c_re, c_im)

    L = S5_CHUNK
    s_idx = jnp.arange(L)[:, None]
    t_idx = jnp.arange(L)[None, :]
    k5 = kall.reshape(2, g, ch, L, ch)
    kf = jnp.take(k5[0], jnp.abs(t_idx - s_idx).reshape(-1), axis=2).reshape(g, ch, L, L, ch)
    kb = jnp.take(k5[1], jnp.abs(t_idx - s_idx).reshape(-1), axis=2).reshape(g, ch, L, L, ch)
    sel = (t_idx >= s_idx)[None, None, :, :, None]
    t_op = jnp.where(sel, kf, kb).transpose(0, 2, 4, 3, 1).reshape(g, n, n)

    v5 = lambda a: a.reshape(2, g, p, L, ch)
    e_f = lambda a: jnp.flip(v5(a)[0], axis=2).transpose(0, 2, 3, 1).reshape(g, n, p)
    e_b = lambda a: v5(a)[1].transpose(0, 2, 3, 1).reshape(g, n, p)
    e_op = jnp.concatenate([e_f(vre), e_b(vre), e_f(vim), e_b(vim)], axis=-1)

    w5 = lambda a: a.reshape(2, g, p, L, ch)
    f_f = lambda a: w5(a)[0].reshape(g, p, n)
    f_b = lambda a: jnp.flip(w5(a)[1], axis=2).reshape(g, p, n)
    f_op = jnp.concatenate([f_f(wre), f_b(wre), f_f(wnim), f_b(wnim)], axis=1)

    lanes = lambda a: jnp.concatenate([a[0, :, :, 0], a[1, :, :, 0]], axis=-1).reshape(g, 1, 2 * p)
    return t_op, e_op, f_op, lanes(are), lanes(aim)


def _s5_group_kernel(x_ref, t_ref, e_ref, f_ref, are_ref, aim_ref, o_ref, s_ref, c_ref, *, nb, n_chunks):
    x = x_ref[0]
    half = e_ref.shape[2] // 2
    s_ref[...] = jnp.dot(x, e_ref[0], preferred_element_type=F32)
    a_re = jnp.broadcast_to(are_ref[0], (nb, half))
    a_im = jnp.broadcast_to(aim_ref[0], (nb, half))
    fwd = lax.broadcasted_iota(jnp.int32, (nb, half), 1) < half // 2

    def step(kk, carry):
        h_re, h_im = carry
        rf = pl.multiple_of(kk * nb, nb)
        rb = pl.multiple_of((n_chunks - 1 - kk) * nb, nb)
        c_ref[pl.ds(rf, nb), 0:half // 2] = h_re[:, 0:half // 2]
        c_ref[pl.ds(rb, nb), half // 2:half] = h_re[:, half // 2:half]
        c_ref[pl.ds(rf, nb), half:half + half // 2] = h_im[:, 0:half // 2]
        c_ref[pl.ds(rb, nb), half + half // 2:] = h_im[:, half // 2:half]
        s_re = jnp.where(fwd, s_ref[pl.ds(rf, nb), 0:half], s_ref[pl.ds(rb, nb), 0:half])
        s_im = jnp.where(fwd, s_ref[pl.ds(rf, nb), half:], s_ref[pl.ds(rb, nb), half:])
        return (a_re * h_re - a_im * h_im + s_re, a_re * h_im + a_im * h_re + s_im)

    zero = jnp.zeros((nb, half), F32)
    lax.fori_loop(0, n_chunks, step, (zero, zero))
    y = jnp.dot(x, t_ref[0], preferred_element_type=F32)
    y = y + jnp.dot(c_ref[...].astype(x.dtype), f_ref[0], preferred_element_type=F32)
    o_ref[0] = y.astype(o_ref.dtype)


def s5_scan(xg, t_op, e_op, f_op, a_re, a_im, *, nb):
    g, rows, n = xg.shape
    grp = lambda *tail: pl.BlockSpec((1,) + tail, lambda i: (i, 0, 0))
    return pl.pallas_call(
        functools.partial(_s5_group_kernel, nb=nb, n_chunks=rows // nb),
        grid=(g,),
        in_specs=[grp(rows, n), grp(n, n), grp(n, n), grp(n, n), grp(1, n // 2), grp(1, n // 2)],
        out_specs=grp(rows, n),
        out_shape=jax.ShapeDtypeStruct((g, rows, n), F32),
        scratch_shapes=[pltpu.VMEM((rows, n), F32), pltpu.VMEM((rows, n), F32)],
        compiler_params=_params(("parallel",)),
        name="s5_scan",
    )(xg, t_op, e_op, f_op, a_re, a_im)


LANES = 128


def _grid_transposes(sets, idx, axis, unit):
    n = len(sets[0])
    size = n * unit
    sets = [list(v) for v in sets]
    d = 1
    while d < n:
        low = (idx & d) == 0
        for v in sets:
            for j in range(n):
                if j & d == 0:
                    a, b = v[j], v[j + d]
                    v[j] = jnp.where(low, a, pltpu.roll(b, d * unit, axis))
                    v[j + d] = jnp.where(low, pltpu.roll(a, size - d * unit, axis), b)
        d *= 2
    return sets


def _inv_rms_to(inv_ref, x_ref):
    x = x_ref[...]
    inv = lax.rsqrt(jnp.mean(x * x, axis=-1, keepdims=True) + RMS_EPS)
    inv_ref[...] = jnp.broadcast_to(inv, inv_ref.shape)


def _s5_in_kernel(x_ref, g_ref, o_ref, inv_ref, *, tb, ch):
    nb, _, d = x_ref.shape
    per = LANES // ch
    assert per == SUBLANES and nb % SUBLANES == 0 and S5_CHUNK % SUBLANES == 0
    _inv_rms_to(inv_ref, x_ref)
    piece = lax.broadcasted_iota(jnp.int32, (nb, LANES), 1) // ch
    sub = lax.broadcasted_iota(jnp.int32, (SUBLANES, LANES), 0)

    def tile_body(lt, carry):
        l0 = pl.multiple_of(lt * LANES, LANES)
        gl = g_ref[:, pl.ds(l0, LANES)]
        n_bb, n_t8 = nb // SUBLANES, S5_CHUNK // SUBLANES
        keys = [(c, bb, t8) for c in range(tb) for bb in range(n_bb) for t8 in range(n_t8)]
        sets = []
        for c, bb, t8 in keys:
            rows = slice(c * S5_CHUNK + t8 * SUBLANES, c * S5_CHUNK + (t8 + 1) * SUBLANES)
            sets.append([x_ref[bb * SUBLANES + b, rows, pl.ds(l0, LANES)] * inv_ref[bb * SUBLANES + b, rows, :] * gl
                         for b in range(SUBLANES)])
        by_step = dict(zip(keys, _grid_transposes(sets, sub, 0, 1)))
        keys = [(c, h) for c in range(tb) for h in range(S5_CHUNK // per)]
        sets = [[jnp.concatenate([by_step[c, bb, (h * per + s) // SUBLANES][(h * per + s) % SUBLANES]
                                  for bb in range(n_bb)], axis=0) for s in range(per)] for c, h in keys]
        for (c, h), w in zip(keys, _grid_transposes(sets, piece, 1, ch)):
            for gg in range(per):
                o_ref[lt * per + gg, c * nb:(c + 1) * nb, h * LANES:(h + 1) * LANES] = w[gg].astype(o_ref.dtype)
        return carry

    lax.fori_loop(0, d // LANES, tile_body, 0)


def s5_to_groups(x3, g, *, ch, tb):
    nb, s, d = x3.shape
    rows = tb * S5_CHUNK
    return pl.pallas_call(
        functools.partial(_s5_in_kernel, tb=tb, ch=ch),
        grid=(s // rows,),
        in_specs=[pl.BlockSpec((nb, rows, d), lambda i: (0, i, 0)), pl.BlockSpec((1, d), lambda i: (0, 0))],
        out_specs=pl.BlockSpec((d // ch, tb * nb, S5_CHUNK * ch), lambda i: (0, i, 0)),
        out_shape=jax.ShapeDtypeStruct((d // ch, (s // S5_CHUNK) * nb, S5_CHUNK * ch), BF16),
        scratch_shapes=[pltpu.VMEM((nb, rows, LANES), F32)],
        compiler_params=_params(("parallel",)),
        name="s5_to_groups",
    )(x3, g.reshape(1, d))


def _s5_out_kernel(yg_ref, x_ref, g_ref, d_ref, o_ref, inv_ref, *, tb, ch):
    nb, _, d = x_ref.shape
    per = LANES // ch
    assert per == SUBLANES and nb % SUBLANES == 0 and S5_CHUNK % SUBLANES == 0
    _inv_rms_to(inv_ref, x_ref)
    piece = lax.broadcasted_iota(jnp.int32, (nb, LANES), 1) // ch
    sub = lax.broadcasted_iota(jnp.int32, (SUBLANES, LANES), 0)

    def tile_body(lt, carry):
        l0 = pl.multiple_of(lt * LANES, LANES)
        gl = g_ref[:, pl.ds(l0, LANES)]
        dl = d_ref[:, pl.ds(l0, LANES)]
        n_bb, n_t8 = nb // SUBLANES, S5_CHUNK // SUBLANES
        keys = [(c, h) for c in range(tb) for h in range(S5_CHUNK // per)]
        sets = [[yg_ref[lt * per + gg, c * nb:(c + 1) * nb, h * LANES:(h + 1) * LANES] for gg in range(per)]
                for c, h in keys]
        by_step = dict(zip(keys, _grid_transposes(sets, piece, 1, ch)))
        keys = [(c, bb, t8) for c in range(tb) for bb in range(n_bb) for t8 in range(n_t8)]
        sets = [[by_step[c, (t8 * SUBLANES + t) // per][(t8 * SUBLANES + t) % per][bb * SUBLANES:(bb + 1) * SUBLANES, :]
                 for t in range(SUBLANES)] for c, bb, t8 in keys]
        by_batch = dict(zip(keys, _grid_transposes(sets, sub, 0, 1)))
        for c in range(tb):
            for bb in range(n_bb):
                for b in range(SUBLANES):
                    acts = []
                    for t8 in range(n_t8):
                        rows = slice(c * S5_CHUNK + t8 * SUBLANES, c * S5_CHUNK + (t8 + 1) * SUBLANES)
                        xn = x_ref[bb * SUBLANES + b, rows, pl.ds(l0, LANES)] * inv_ref[bb * SUBLANES + b, rows, :] * gl
                        u = by_batch[c, bb, t8][b] + dl * xn
                        acts.append(0.5 * u * (1.0 + jnp.tanh(math.sqrt(2.0 / math.pi) * (u + 0.044715 * (u * u * u)))))
                    o_ref[bb * SUBLANES + b, c * S5_CHUNK:(c + 1) * S5_CHUNK, pl.ds(l0, LANES)] = (
                        jnp.concatenate(acts, axis=0).astype(o_ref.dtype))
        return carry

    lax.fori_loop(0, d // LANES, tile_body, 0)


def s5_from_groups(yg, x3, g, d_skip, *, ch, tb):
    nb, s, d = x3.shape
    rows = tb * S5_CHUNK
    return pl.pallas_call(
        functools.partial(_s5_out_kernel, tb=tb, ch=ch),
        grid=(s // rows,),
        in_specs=[pl.BlockSpec((d // ch, tb * nb, S5_CHUNK * ch), lambda i: (0, i, 0)),
                  pl.BlockSpec((nb, rows, d), lambda i: (0, i, 0)),
                  pl.BlockSpec((1, d), lambda i: (0, 0)),
                  pl.BlockSpec((1, d), lambda i: (0, 0))],
        out_specs=pl.BlockSpec((nb, rows, d), lambda i: (0, i, 0)),
        out_shape=jax.ShapeDtypeStruct((nb, s, d), BF16),
        scratch_shapes=[pltpu.VMEM((nb, rows, LANES), F32)],
        compiler_params=_params(("parallel",)),
        name="s5_from_groups",
    )(yg, x3, g.reshape(1, d), d_skip.reshape(1, d))


def _s5_glu_kernel(a_ref, x_ref, wv_ref, wg_ref, o_ref):
    a = a_ref[...]
    val = jnp.dot(a, wv_ref[...], preferred_element_type=F32)
    gate = jnp.dot(a, wg_ref[...], preferred_element_type=F32)
    o_ref[...] = x_ref[...] + val * jax.nn.sigmoid(gate)


def s5_glu(act, x, w_val, w_gate, *, tm):
    m, d = x.shape
    return pl.pallas_call(
        _s5_glu_kernel,
        grid=(m // tm,),
        in_specs=[pl.BlockSpec((tm, d), lambda i: (i, 0)),
                  pl.BlockSpec((tm, d), lambda i: (i, 0)),
                  _resident(w_val.shape), _resident(w_gate.shape)],
        out_specs=pl.BlockSpec((tm, d), lambda i: (i, 0)),
        out_shape=jax.ShapeDtypeStruct((m, d), F32),
        compiler_params=_params(("parallel",)),
        name="s5_glu",
    )(act, x, w_val, w_gate)


def _mixer_conv_diff(x, norm_g, w_in, conv_dw, conv_db, conv_ln_g, conv_ln_b, lq1, lk1, lq2, lk2, subln_g, w_out,
                     *, batch, seq, lambda_init):
    d_conv = conv_dw.shape[1]
    d_diff = (w_in.shape[1] - 2 * d_conv) // 3
    z = norm_matmul(x, norm_g, w_in, tm=1024, tn=1280)
    a_out = conformer_conv(z, conv_dw, conv_db, conv_ln_g, conv_ln_b, seq=seq, tr=256)
    b_out = diff_attention(z, lq1, lk1, lq2, lk2, subln_g, batch=batch, seq=seq,
                           q_col=2 * d_conv, k_col=2 * d_conv + d_diff, v_col=2 * d_conv + 2 * d_diff,
                           lambda_init=lambda_init, tq=512)
    return matmul_residual([a_out, b_out], w_out, x, tm=512)


def _mixer_s5(x, norm_g, lam_re, lam_im, log_dt, b_re, b_im, c_re, c_im, d_skip, w_val, w_gate, *, batch, seq):
    m, d = x.shape
    ch = b_re.shape[-1]
    t_op, e_op, f_op, a_re, a_im = s5_operators(lam_re, lam_im, log_dt, b_re, b_im, c_re, c_im, gb=8)
    x3 = x.reshape(batch, seq, d)
    xg = s5_to_groups(x3, norm_g, ch=ch, tb=4)
    yg = s5_scan(xg, t_op, e_op, f_op, a_re, a_im, nb=batch)
    act = s5_from_groups(yg, x3, norm_g, d_skip, ch=ch, tb=4).reshape(m, d)
    return s5_glu(act, x, w_val, w_gate, tm=512)


def _cross_attn_block(x, memf, norm_g, mem_g, wq, wk, wv, wo, *, batch, seq, mem_len):
    d = x.shape[1]
    q = norm_matmul(x, norm_g, wq, tm=512, tn=d)
    kv = norm_matmul(memf, mem_g, jnp.concatenate([wk, wv], axis=1), tm=1024, tn=1024)
    o = cross_attention(q, kv, batch=batch, seq=seq, mem_len=mem_len, tq=2048)
    return matmul_residual([o], wo, x, tm=512)


def kernel(x, mem, norm_mix_g, ab_w_in, conv_dw, conv_db, conv_ln_g, conv_ln_b, diff_lq1, diff_lk1, diff_lq2, diff_lk2, diff_subln_g, ab_w_out, s5_lam_re, s5_lam_im, s5_log_dt, s5_b_re, s5_b_im, s5_c_re, s5_c_im, s5_d, s5_w_val, s5_w_gate, norm_xa_g, norm_mem_g, xa_wq, xa_wk, xa_wv, xa_wo, norm_ffn_g, ffn_w_up, ffn_dw, ffn_db, ffn_w_down, final_g):
    batch, seq, d = x.shape
    mem_len = mem.shape[1]
    depth = norm_mix_g.shape[0]
    xf = x.reshape(batch * seq, d)
    memf = mem.reshape(batch * mem_len, d)
    for layer in range(depth):
        i = layer // 2
        if layer % 2 == 0:
            lambda_init = 0.8 - 0.6 * math.exp(-0.3 * layer)
            xf = _mixer_conv_diff(xf, norm_mix_g[layer], layer_bf16(ab_w_in, i), conv_dw[i], conv_db[i], conv_ln_g[i],
                                  conv_ln_b[i], diff_lq1[i], diff_lk1[i], diff_lq2[i], diff_lk2[i],
                                  diff_subln_g[i], layer_bf16(ab_w_out, i), batch=batch, seq=seq,
                                  lambda_init=lambda_init)
        else:
            xf = _mixer_s5(xf, norm_mix_g[layer], s5_lam_re[i], s5_lam_im[i], s5_log_dt[i], s5_b_re[i],
                           s5_b_im[i], s5_c_re[i], s5_c_im[i], s5_d[i], layer_bf16(s5_w_val, i),
                           layer_bf16(s5_w_gate, i), batch=batch, seq=seq)
        xf = _cross_attn_block(xf, memf, norm_xa_g[layer], norm_mem_g[layer], layer_bf16(xa_wq, layer),
                               layer_bf16(xa_wk, layer), layer_bf16(xa_wv, layer), layer_bf16(xa_wo, layer),
                               batch=batch, seq=seq, mem_len=mem_len)
        xf = conv_ffn_block(xf, norm_ffn_g[layer], layer_bf16(ffn_w_up, layer), ffn_dw[layer], ffn_db[layer],
                            layer_bf16(ffn_w_down, layer), final_g, seq=seq, tm=512, tf=512,
                            final_norm=(layer == depth - 1))
    return xf.reshape(batch, seq, d)
```
